```python
import jax, jax.numpy as jnp
from jax import lax
import numpy as np

D_MODEL = 1024
BATCH = 4
SEQ = 4096
DEPTH = 4

GRID_W = 64
CTX_LEN = 256
N_BRANCH = 3
BRANCH_WIDTH = D_MODEL // 2
RET_HEADS = 4
RET_HEAD_DIM = BRANCH_WIDTH // RET_HEADS
RET_WIDTH = RET_HEADS * RET_HEAD_DIM
RET_CHUNK = 128
LRU_WIDTH = BRANCH_WIDTH
LRU_BLOCKS = 8
LRU_BLOCK = LRU_WIDTH // LRU_BLOCKS
LRU_CONV = 4
LRU_CONV_PAD_LEFT = 1
LRU_C = 8.0
ATT_HEAD_DIM = 64
ATT_Q_HEADS = BRANCH_WIDTH // ATT_HEAD_DIM
ATT_KV_HEADS = 2
ATT_GROUP = ATT_Q_HEADS // ATT_KV_HEADS
ATT_WIDTH = ATT_Q_HEADS * ATT_HEAD_DIM
ATT_KV_WIDTH = ATT_KV_HEADS * ATT_HEAD_DIM
Q_BLOCK = 128
ROPE_THETA = 10000.0
D_FF = 256 * ((8 * D_MODEL // 3 + 255) // 256)
MACARON_WEIGHT = 0.5
N_SUB = 3
NORM_EPS = 1e-6
IN_SIZES = (RET_WIDTH, RET_WIDTH, RET_WIDTH, RET_WIDTH, LRU_WIDTH, LRU_WIDTH,
            ATT_WIDTH, ATT_KV_WIDTH, ATT_KV_WIDTH, N_BRANCH * D_MODEL)
D_IN = sum(IN_SIZES)

kernel_name = 'hybrid_retention_rglru_gqa_prefix_dit'


def _rms(x, g):
    xf = x.astype(jnp.float32)
    y = xf * lax.rsqrt(jnp.mean(xf * xf, axis=-1, keepdims=True) + NORM_EPS)
    return (y * g.astype(jnp.float32)).astype(x.dtype)


def _modulate(x, g, shift, scale):
    return _rms(x, g) * (1 + scale) + shift


def _swiglu(h, w_in, w_out):
    a, b = jnp.split(h @ w_in, 2, axis=-1)
    return (jax.nn.silu(a) * b) @ w_out


def _half_ffn(s, m, g, w_in, w_out):
    return MACARON_WEIGHT * m[:, 2] * _swiglu(_modulate(s, g, m[:, 0], m[:, 1]), w_in, w_out)


def _split_cols(z):
    offs = []
    acc = 0
    for s in IN_SIZES[:-1]:
        acc += s
        offs.append(acc)
    return jnp.split(z, offs, axis=-1)


def _heads(t, n_heads):
    return t.reshape(*t.shape[:-1], n_heads, t.shape[-1] // n_heads)


def _flip(t):
    return jnp.flip(t, axis=1)


def _same(t):
    return t


def _rope_half(x, ang):
    x1, x2 = jnp.split(x, 2, axis=-1)
    cos = jnp.cos(ang)[None, :, None, :].astype(x.dtype)
    sin = jnp.sin(ang)[None, :, None, :].astype(x.dtype)
    return jnp.concatenate([x1 * cos - x2 * sin, x1 * sin + x2 * cos], axis=-1)


def _axial_rope(x, rows, cols):
    half = x.shape[-1] // 2
    freqs = ROPE_THETA ** (-jnp.arange(0, half, 2, dtype=jnp.float32) / half)
    xr, xc = jnp.split(x, 2, axis=-1)
    return jnp.concatenate([_rope_half(xr, rows[:, None] * freqs[None]),
                            _rope_half(xc, cols[:, None] * freqs[None])], axis=-1)


def _retention_chunks(q, k, v, log_gamma, s0):
    B, L, H, _ = q.shape
    dv = v.shape[-1]
    n = L // RET_CHUNK
    pos = jnp.arange(RET_CHUNK, dtype=jnp.float32)
    diff = pos[:, None] - pos[None, :]
    lg = log_gamma.astype(jnp.float32)
    intra = jnp.where(diff[None] >= 0,
                      jnp.exp(jnp.maximum(diff, 0.0)[None] * lg[:, None, None]), 0.0).astype(q.dtype)
    q_decay = jnp.exp((pos[:, None] + 1.0) * lg[None]).astype(q.dtype)
    k_decay = jnp.exp((RET_CHUNK - 1.0 - pos)[:, None] * lg[None]).astype(q.dtype)
    s_decay = jnp.exp(RET_CHUNK * lg).astype(q.dtype)

    def to_chunks(t):
        return t.reshape(B, n, RET_CHUNK, *t.shape[2:]).swapaxes(0, 1)

    def step(s, blk):
        qc, kc, vc = blk
        scores = jnp.einsum('bihd,bjhd->bhij', qc, kc) * intra
        inner = jnp.einsum('bhij,bjhe->bihe', scores, vc)
        cross = jnp.einsum('bihd,bhde->bihe', qc, s) * q_decay[None, :, :, None]
        s_new = s * s_decay[None, :, None, None] + jnp.einsum(
            'bjhd,bjhe->bhde', kc * k_decay[None, :, :, None], vc)
        return s_new, inner + cross

    s_fin, out = lax.scan(step, s0, (to_chunks(q), to_chunks(k), to_chunks(v)))
    return out.swapaxes(0, 1).reshape(B, L, H, dv), s_fin


def _bidir_retention(qc, kc, vc, ql, kl, vl, log_gamma):
    s0 = jnp.zeros((ql.shape[0], RET_HEADS, RET_HEAD_DIM, RET_HEAD_DIM), ql.dtype)
    outs_c, outs_l = [], []
    for d in range(2):
        f = _flip if d else _same
        oc, sc = _retention_chunks(f(qc), f(kc), f(vc), log_gamma[d], s0)
        ol, _ = _retention_chunks(f(ql), f(kl), f(vl), log_gamma[d], sc)
        outs_c.append(f(oc))
        outs_l.append(f(ol))
    return outs_c[0] + outs_c[1], outs_l[0] + outs_l[1]


def _head_norm(y, g):
    yf = y.astype(jnp.float32)
    mu = jnp.mean(yf, axis=-1, keepdims=True)
    var = jnp.mean(jnp.square(yf - mu), axis=-1, keepdims=True)
    yn = ((yf - mu) * lax.rsqrt(var + NORM_EPS)).reshape(*y.shape[:2], -1)
    return (yn * g.astype(jnp.float32)).astype(y.dtype)


def _conv_centred(x, w, b):
    L = x.shape[1]
    xp = jnp.pad(x, ((0, 0), (LRU_CONV_PAD_LEFT, LRU_CONV - 1 - LRU_CONV_PAD_LEFT), (0, 0)))
    y = b
    for j in range(LRU_CONV):
        y = y + xp[:, j:j + L] * w[j]
    return y


def _block_diag(x, w, b):
    B, L, _ = x.shape
    y = jnp.einsum('blnc,ncd->blnd', x.reshape(B, L, LRU_BLOCKS, LRU_BLOCK), w)
    return y.reshape(B, L, LRU_WIDTH) + b


def _rglru_scan(x, w_a, b_a, w_x, b_x, lam, h0):
    r = jax.nn.sigmoid(_block_diag(x, w_a, b_a))
    i = jax.nn.sigmoid(_block_diag(x, w_x, b_x))
    log_a = (-LRU_C * r * jax.nn.softplus(-lam)).astype(jnp.float32)
    a = jnp.exp(log_a)
    u = jnp.sqrt(-jnp.expm1(2.0 * log_a)) * (i * x).astype(jnp.float32)

    def combine(p, q):
        a1, b1 = p
        a2, b2 = q
        return a1 * a2, a2 * b1 + b2

    a_cum, b_cum = lax.associative_scan(combine, (a, u), axis=1)
    h = b_cum + a_cum * h0[:, None]
    return h, h[:, -1]


def _bidir_rglru(xc, xl, w_a, b_a, w_x, b_x, lam):
    h0 = jnp.zeros((xl.shape[0], LRU_WIDTH), jnp.float32)
    outs_c, outs_l = [], []
    for d in range(2):
        f = _flip if d else _same
        hc, sc = _rglru_scan(f(xc), w_a[d], b_a[d], w_x[d], b_x[d], lam[d], h0)
        hl, _ = _rglru_scan(f(xl), w_a[d], b_a[d], w_x[d], b_x[d], lam[d], sc)
        outs_c.append(f(hc))
        outs_l.append(f(hl))
    return (outs_c[0] + outs_c[1]).astype(xc.dtype), (outs_l[0] + outs_l[1]).astype(xl.dtype)


def _attend(q, k, v):
    s = jnp.einsum('bqkgd,bskd->bkgqs', q, k).astype(jnp.float32) * (ATT_HEAD_DIM ** -0.5)
    p = jax.nn.softmax(s, axis=-1).astype(v.dtype)
    return jnp.einsum('bkgqs,bskd->bqkgd', p, v)


def _attend_blocks(q, k, v):
    B, T = q.shape[:2]
    nb = T // Q_BLOCK
    qb = q.reshape(B, nb, Q_BLOCK, ATT_KV_HEADS, ATT_GROUP, ATT_HEAD_DIM).swapaxes(0, 1)
    out = lax.map(lambda blk: _attend(blk, k, v), qb)
    return out.swapaxes(0, 1).reshape(B, T, ATT_WIDTH)


def _merge(y_ret, y_lru, y_att, gate_logits, w_branch, w_out):
    y = jnp.stack([y_ret, y_lru, y_att], axis=2)
    u = jnp.einsum('blnw,nwd->blnd', y, w_branch)
    g = jax.nn.sigmoid(gate_logits.reshape(*gate_logits.shape[:2], N_BRANCH, D_MODEL))
    return jnp.sum(g * u, axis=2) @ w_out


def _token_mix(hc, hl, p, rows, cols, ctx_out):
    rq_c, rk_c, rv_c, rg_c, lx_c, lz_c, aq_c, ak_c, av_c, gt_c = _split_cols(hc @ p['w_in'])
    rq_l, rk_l, rv_l, rg_l, lx_l, lz_l, aq_l, ak_l, av_l, gt_l = _split_cols(hl @ p['w_in'])
    k_scale = RET_HEAD_DIM ** -0.5
    ret_c, ret_l = _bidir_retention(
        _heads(rq_c, RET_HEADS), _heads(rk_c, RET_HEADS) * k_scale, _heads(rv_c, RET_HEADS),
        _axial_rope(_heads(rq_l, RET_HEADS), rows, cols),
        _axial_rope(_heads(rk_l, RET_HEADS) * k_scale, rows, cols),
        _heads(rv_l, RET_HEADS),
        jax.nn.log_sigmoid(p['ret_decay_logit']))
    lru_c, lru_l = _bidir_rglru(
        _conv_centred(lx_c, p['lru_conv_w'], p['lru_conv_b']),
        _conv_centred(lx_l, p['lru_conv_w'], p['lru_conv_b']),
        p['lru_w_a'], p['lru_b_a'], p['lru_w_x'], p['lru_b_x'], p['lru_lambda'])
    qc = _rms(_heads(aq_c, ATT_Q_HEADS), p['q_norm_g'])
    kc = _rms(_heads(ak_c, ATT_KV_HEADS), p['k_norm_g'])
    vc = _heads(av_c, ATT_KV_HEADS)
    ql = _axial_rope(_rms(_heads(aq_l, ATT_Q_HEADS), p['q_norm_g']), rows, cols)
    kl = _axial_rope(_rms(_heads(ak_l, ATT_KV_HEADS), p['k_norm_g']), rows, cols)
    vl = _heads(av_l, ATT_KV_HEADS)
    att_l = _attend_blocks(ql, jnp.concatenate([kc, kl], axis=1), jnp.concatenate([vc, vl], axis=1))
    y_l = _merge(_head_norm(ret_l, p['ret_norm_g']) * jax.nn.silu(rg_l),
                 jax.nn.gelu(lz_l) * lru_l, att_l, gt_l, p['w_branch'], p['w_out'])
    if not ctx_out:
        return y_l, None
    B, Lc = hc.shape[:2]
    att_c = _attend(qc.reshape(B, Lc, ATT_KV_HEADS, ATT_GROUP, ATT_HEAD_DIM), kc, vc).reshape(B, Lc, ATT_WIDTH)
    y_c = _merge(_head_norm(ret_c, p['ret_norm_g']) * jax.nn.silu(rg_c),
                 jax.nn.gelu(lz_c) * lru_c, att_c, gt_c, p['w_branch'], p['w_out'])
    return y_l, y_c


def setup_inputs(seed: int = 0) -> dict:
    key = jax.random.key(seed)
    ks = jax.random.split(key, 24)
    f32 = jnp.float32
    D = D_MODEL

    def nrm(k, shape, scale):
        return jax.random.normal(k, shape, f32) * scale

    gamma = 1.0 - 2.0 ** (-5.0 - jnp.arange(RET_HEADS, dtype=f32))
    ret_logit0 = jnp.log(gamma) - jnp.log1p(-gamma)
    u = jax.random.uniform(ks[12], (DEPTH, 2, LRU_WIDTH), f32, 0.9, 0.999)
    a0 = u ** (1.0 / LRU_C)
    lam = jnp.log(a0) - jnp.log1p(-a0)
    return {
        'x': nrm(ks[0], (BATCH, SEQ, D), 1.0),
        'c': nrm(ks[1], (BATCH, D), 1.0),
        'ctx': nrm(ks[2], (BATCH, CTX_LEN, D), 1.0),
        'c_ctx': nrm(ks[3], (D,), 1.0),
        'w_mod': nrm(ks[4], (DEPTH, D, N_SUB * 3 * D), 0.5 * D ** -0.5),
        'b_mod': nrm(ks[5], (DEPTH, N_SUB * 3 * D), 0.02),
        'norm_g': 1.0 + nrm(ks[6], (DEPTH, N_SUB, D), 0.02),
        'ffn_w_in': nrm(ks[7], (DEPTH, 2, D, 2 * D_FF), D ** -0.5),
        'ffn_w_out': nrm(ks[8], (DEPTH, 2, D_FF, D), D_FF ** -0.5),
        'w_in': nrm(ks[9], (DEPTH, D, D_IN), D ** -0.5),
        'ret_decay_logit': ret_logit0 + nrm(ks[10], (DEPTH, 2, RET_HEADS), 0.1),
        'ret_norm_g': 1.0 + nrm(ks[11], (DEPTH, RET_WIDTH), 0.02),
        'lru_conv_w': nrm(ks[13], (DEPTH, LRU_CONV, LRU_WIDTH), LRU_CONV ** -0.5),
        'lru_conv_b': nrm(ks[14], (DEPTH, LRU_WIDTH), 0.02),
        'lru_w_a': nrm(ks[15], (DEPTH, 2, LRU_BLOCKS, LRU_BLOCK, LRU_BLOCK), LRU_BLOCK ** -0.5),
        'lru_b_a': nrm(ks[16], (DEPTH, 2, LRU_WIDTH), 0.1),
        'lru_w_x': nrm(ks[17], (DEPTH, 2, LRU_BLOCKS, LRU_BLOCK, LRU_BLOCK), LRU_BLOCK ** -0.5),
        'lru_b_x': nrm(ks[18], (DEPTH, 2, LRU_WIDTH), 0.1),
        'lru_lambda': lam,
        'attn_q_norm_g': 1.0 + nrm(ks[19], (DEPTH, ATT_HEAD_DIM), 0.02),
        'attn_k_norm_g': 1.0 + nrm(ks[20], (DEPTH, ATT_HEAD_DIM), 0.02),
        'w_branch': nrm(ks[21], (DEPTH, N_BRANCH, BRANCH_WIDTH, D), BRANCH_WIDTH ** -0.5),
        'w_out': nrm(ks[22], (DEPTH, D, D), D ** -0.5),
        'final_norm_g': 1.0 + nrm(ks[23], (D,), 0.02),
    }


def reference(x, c, ctx, c_ctx, w_mod, b_mod, norm_g, ffn_w_in, ffn_w_out, w_in,
              ret_decay_logit, ret_norm_g, lru_conv_w, lru_conv_b, lru_w_a, lru_b_a,
              lru_w_x, lru_b_x, lru_lambda, attn_q_norm_g, attn_k_norm_g, w_branch,
              w_out, final_norm_g):
    seq = x.shape[1]
    n_rows = seq // GRID_W
    rows = jnp.repeat(jnp.arange(n_rows, dtype=jnp.float32), GRID_W)
    cols = jnp.tile(jnp.arange(GRID_W, dtype=jnp.float32), n_rows)
    s_lat = jax.nn.silu(c)
    s_ctx = jax.nn.silu(c_ctx)[None]
    xc = ctx
    for l in range(DEPTH):
        last = l == DEPTH - 1
        m_l = (s_lat @ w_mod[l] + b_mod[l]).reshape(-1, N_SUB, 3, 1, D_MODEL)
        m_c = (s_ctx @ w_mod[l] + b_mod[l]).reshape(1, N_SUB, 3, 1, D_MODEL)
        x = x + _half_ffn(x, m_l[:, 0], norm_g[l, 0], ffn_w_in[l, 0], ffn_w_out[l, 0])
        xc = xc + _half_ffn(xc, m_c[:, 0], norm_g[l, 0], ffn_w_in[l, 0], ffn_w_out[l, 0])
        p = {
            'w_in': w_in[l], 'ret_decay_logit': ret_decay_logit[l], 'ret_norm_g': ret_norm_g[l],
            'lru_conv_w': lru_conv_w[l], 'lru_conv_b': lru_conv_b[l],
            'lru_w_a': lru_w_a[l], 'lru_b_a': lru_b_a[l], 'lru_w_x': lru_w_x[l], 'lru_b_x': lru_b_x[l],
            'lru_lambda': lru_lambda[l], 'q_norm_g': attn_q_norm_g[l], 'k_norm_g': attn_k_norm_g[l],
            'w_branch': w_branch[l], 'w_out': w_out[l],
        }
        h_l = _modulate(x, norm_g[l, 1], m_l[:, 1, 0], m_l[:, 1, 1])
        h_c = _modulate(xc, norm_g[l, 1], m_c[:, 1, 0], m_c[:, 1, 1])
        y_l, y_c = _token_mix(h_c, h_l, p, rows, cols, not last)
        x = x + m_l[:, 1, 2] * y_l
        x = x + _half_ffn(x, m_l[:, 2], norm_g[l, 2], ffn_w_in[l, 1], ffn_w_out[l, 1])
        if not last:
            xc = xc + m_c[:, 1, 2] * y_c
            xc = xc + _half_ffn(xc, m_c[:, 2], norm_g[l, 2], ffn_w_in[l, 1], ffn_w_out[l, 1])
    return _rms(x, final_norm_g)
```

```python
import functools

import jax
import jax.numpy as jnp
from jax import lax
from jax.experimental import pallas as pl
from jax.experimental.pallas import tpu as pltpu

D_MODEL = 1024
GRID_W = 64
CTX_LEN = 256
N_BRANCH = 3
BRANCH_WIDTH = 512
RET_HEADS = 4
RET_HEAD_DIM = 128
RET_CHUNK = 128
LRU_WIDTH = 512
LRU_BLOCKS = 8
LRU_BLOCK = 64
LRU_C = 8.0
ATT_HEAD_DIM = 64
ATT_Q_HEADS = 8
ATT_KV_HEADS = 2
ATT_GROUP = 4
ROPE_THETA = 10000.0
D_FF = 2816
MACARON_WEIGHT = 0.5
N_SUB = 3
NORM_EPS = 1e-6
MIX_COLS = 4 * BRANCH_WIDTH + 2 * LRU_WIDTH + BRANCH_WIDTH + 2 * ATT_HEAD_DIM * ATT_KV_HEADS

LANES = 128
SUBLANES = 8
VMEM_LIMIT = 56 * 1024 * 1024

ROW_TILE = 256
FFN_CHUNK = 256
LRU_CHUNK = 256

BF16 = jnp.bfloat16
F32 = jnp.float32


def _cparams(*sem):
    return pltpu.CompilerParams(dimension_semantics=sem, vmem_limit_bytes=VMEM_LIMIT)


def _resident(shape):
    nd = len(shape)
    return pl.BlockSpec(shape, lambda *_: (0,) * nd, pipeline_mode=pl.Buffered(1))


def _modulated_norm(x, g, shift, scale):
    ms = jnp.mean(x * x, axis=-1, keepdims=True)
    return x * lax.rsqrt(ms + NORM_EPS) * g * (1.0 + scale) + shift


def _mod_kernel(c_ref, w_ref, b_ref, o_ref):
    c = c_ref[...]
    s = (c * jax.nn.sigmoid(c)).astype(BF16)
    o_ref[...] = jnp.dot(s, w_ref[...].astype(BF16), preferred_element_type=F32) + b_ref[...]


def _modulation(c8, w_mod, b_mod):
    depth, d, n = w_mod.shape
    tn = 1024
    return pl.pallas_call(
        _mod_kernel,
        grid=(depth, n // tn),
        in_specs=[
            pl.BlockSpec((SUBLANES, d), lambda l, j: (0, 0)),
            pl.BlockSpec((None, d, tn), lambda l, j: (l, 0, j)),
            pl.BlockSpec((None, 1, tn), lambda l, j: (l, 0, j)),
        ],
        out_specs=pl.BlockSpec((None, SUBLANES, tn), lambda l, j: (l, 0, j)),
        out_shape=jax.ShapeDtypeStruct((depth, SUBLANES, n), F32),
        compiler_params=_cparams("parallel", "parallel"),
        name="adaln_modulation",
    )(c8, w_mod, b_mod.reshape(depth, 1, n))


def _ffn_kernel(x_ref, mod_ref, g_ref, w1_ref, w2_ref, *rest, sub, final_norm):
    if final_norm:
        fg_ref, o_ref = rest
    else:
        (o_ref,) = rest
    x = x_ref[...]
    shift = mod_ref[3 * sub + 0:3 * sub + 1, :]
    scale = mod_ref[3 * sub + 1:3 * sub + 2, :]
    gate = mod_ref[3 * sub + 2:3 * sub + 3, :]
    hb = _modulated_norm(x, g_ref[...], shift, scale).astype(BF16)
    acc = jnp.zeros(x.shape, F32)
    for c in range(D_FF // FFN_CHUNK):
        lo = c * FFN_CHUNK
        a = jnp.dot(hb, w1_ref[:, lo:lo + FFN_CHUNK], preferred_element_type=F32)
        b = jnp.dot(hb, w1_ref[:, D_FF + lo:D_FF + lo + FFN_CHUNK], preferred_element_type=F32)
        act = (a * jax.nn.sigmoid(a) * b).astype(BF16)
        acc = acc + jnp.dot(act, w2_ref[lo:lo + FFN_CHUNK, :], preferred_element_type=F32)
    y = x + (MACARON_WEIGHT * gate) * acc
    if final_norm:
        ms = jnp.mean(y * y, axis=-1, keepdims=True)
        y = y * lax.rsqrt(ms + NORM_EPS) * fg_ref[...]
    o_ref[...] = y


def _mod_row(t):
    return jnp.minimum(t // (CTX_LEN // ROW_TILE), 1)


def _ffn(xs, mod, g, w1, w2, *, sub, tile_off=0, final_g=None):
    bsz, t_all, d = xs.shape
    nt = t_all // ROW_TILE - tile_off
    in_specs = [
        pl.BlockSpec((None, ROW_TILE, d), lambda b, t: (b, t + tile_off, 0)),
        pl.BlockSpec((None, 3 * N_SUB, d), lambda b, t: (2 * b + _mod_row(t + tile_off), 0, 0)),
        _resident((1, d)),
        _resident(w1.shape),
        _resident(w2.shape),
    ]
    args = [xs, mod, g, w1, w2]
    if final_g is not None:
        in_specs.append(_resident((1, d)))
        args.append(final_g)
    return pl.pallas_call(
        functools.partial(_ffn_kernel, sub=sub, final_norm=final_g is not None),
        grid=(bsz, nt),
        in_specs=in_specs,
        out_specs=pl.BlockSpec((None, ROW_TILE, d), lambda b, t: (b, t, 0)),
        out_shape=jax.ShapeDtypeStruct((bsz, nt * ROW_TILE, d), F32),
        compiler_params=_cparams("parallel", "parallel"),
        name="swiglu_half_step",
    )(*args)


def _rope_block(x, cos, sin_signed, half):
    lane = lax.broadcasted_iota(jnp.int32, x.shape, 1)
    up = pltpu.roll(x, LANES - half, axis=1)
    down = pltpu.roll(x, half, axis=1)
    partner = jnp.where((lane & half) == 0, up, down)
    return x * cos + partner * sin_signed


def _head_pair_rms(x, g):
    lane = lax.broadcasted_iota(jnp.int32, x.shape, 1)
    first = lane < ATT_HEAD_DIM
    sq = x * x
    s0 = jnp.sum(jnp.where(first, sq, 0.0), axis=-1, keepdims=True)
    s1 = jnp.sum(jnp.where(first, 0.0, sq), axis=-1, keepdims=True)
    ms = jnp.where(first, s0, s1) * (1.0 / ATT_HEAD_DIM)
    return x * lax.rsqrt(ms + NORM_EPS) * g


def _proj_kernel(x_ref, mod_ref, g_ref, w_ref, rcos_ref, rsin_ref, acos_ref, asin_ref,
                 qg_ref, kg_ref,
                 rq_ref, rk_ref, rv_ref, rg_ref, lx_ref, lz_ref, aq_ref, ak_ref, av_ref):
    x = x_ref[...]
    hb = _modulated_norm(x, g_ref[...], mod_ref[3:4, :], mod_ref[4:5, :]).astype(BF16)
    w = BRANCH_WIDTH

    def cols(lo, n):
        return jnp.dot(hb, w_ref[:, lo:lo + n], preferred_element_type=F32)

    rcos, rsin = rcos_ref[...], rsin_ref[...]
    acos, asin = acos_ref[...], asin_ref[...]
    k_scale = RET_HEAD_DIM ** -0.5
    zq = cols(0, w)
    zk = cols(w, w)
    for h in range(RET_HEADS):
        sl = slice(h * LANES, (h + 1) * LANES)
        rq_ref[:, sl] = _rope_block(zq[:, sl], rcos, rsin, RET_HEAD_DIM // 4).astype(BF16)
        rk_ref[:, sl] = (_rope_block(zk[:, sl], rcos, rsin, RET_HEAD_DIM // 4) * k_scale).astype(BF16)
    rv_ref[...] = cols(2 * w, w).astype(BF16)
    rg_ref[...] = cols(3 * w, w)
    lx_ref[...] = cols(4 * w, w)
    lz_ref[...] = cols(5 * w, w)
    za = cols(6 * w, w)
    q_scale = ATT_HEAD_DIM ** -0.5
    for p in range(ATT_Q_HEADS // 2):
        sl = slice(p * LANES, (p + 1) * LANES)
        qn = _head_pair_rms(za[:, sl], qg_ref[...])
        aq_ref[:, sl] = (_rope_block(qn, acos, asin, ATT_HEAD_DIM // 4) * q_scale).astype(BF16)
    zkv = cols(7 * w, 2 * LANES)
    kn = _rope_block(_head_pair_rms(zkv[:, :LANES], kg_ref[...]), acos, asin, ATT_HEAD_DIM // 4)
    for j in range(ATT_KV_HEADS):
        ak_ref[j] = kn[:, j * ATT_HEAD_DIM:(j + 1) * ATT_HEAD_DIM].astype(BF16)
        av_ref[j] = zkv[:, LANES + j * ATT_HEAD_DIM:LANES + (j + 1) * ATT_HEAD_DIM].astype(BF16)


def _proj(xs, mod, g, w_mix, rope, qg, kg):
    bsz, t_all, d = xs.shape
    nt = t_all // ROW_TILE
    w = BRANCH_WIDTH
    tile = lambda n: pl.BlockSpec((None, ROW_TILE, n), lambda b, t: (b, t, 0))
    table = pl.BlockSpec((ROW_TILE, LANES), lambda b, t: (t, 0))
    kv_spec = pl.BlockSpec((None, ATT_KV_HEADS, ROW_TILE, ATT_HEAD_DIM), lambda b, t: (b, 0, t, 0))
    sds = lambda n, dt: jax.ShapeDtypeStruct((bsz, t_all, n), dt)
    kv_sds = jax.ShapeDtypeStruct((bsz, ATT_KV_HEADS, t_all, ATT_HEAD_DIM), BF16)
    return pl.pallas_call(
        _proj_kernel,
        grid=(bsz, nt),
        in_specs=[
            tile(d),
            pl.BlockSpec((None, 3 * N_SUB, d), lambda b, t: (2 * b + _mod_row(t), 0, 0)),
            _resident((1, d)),
            _resident(w_mix.shape),
            table, table, table, table,
            _resident((1, LANES)), _resident((1, LANES)),
        ],
        out_specs=[tile(w), tile(w), tile(w), tile(w), tile(w), tile(w), tile(w), kv_spec, kv_spec],
        out_shape=[sds(w, BF16), sds(w, BF16), sds(w, BF16), sds(w, F32), sds(w, F32), sds(w, F32),
                   sds(w, BF16), kv_sds, kv_sds],
        compiler_params=_cparams("parallel", "parallel"),
        name="mixer_input_projection",
    )(xs, mod, g, w_mix, *rope, qg, kg)


def _log_sigmoid(x):
    return jnp.minimum(x, 0.0) - jnp.log1p(jnp.exp(-jnp.abs(x)))


def _retention_kernel(q_ref, k_ref, v_ref, rg_ref, dl_ref, ng_ref, o_ref, of_ref, ob_ref):
    h = pl.program_id(1)
    c = RET_CHUNK
    n_chunks = q_ref.shape[0] // c
    n_ctx = CTX_LEN // c
    lg_f = _log_sigmoid(dl_ref[pl.ds(h, 1), :])
    lg_b = _log_sigmoid(dl_ref[pl.ds(RET_HEADS + h, 1), :])
    row = lax.broadcasted_iota(jnp.int32, (c, c), 0).astype(F32)
    col = lax.broadcasted_iota(jnp.int32, (c, c), 1).astype(F32)
    diff = row - col
    intra_f = jnp.where(diff >= 0, jnp.exp(jnp.maximum(diff, 0.0) * lg_f), 0.0)
    intra_b = jnp.where(diff <= 0, jnp.exp(jnp.maximum(-diff, 0.0) * lg_b), 0.0)
    qd_f = jnp.exp((row + 1.0) * lg_f)
    kd_f = jnp.exp((c - 1.0 - row) * lg_f)
    qd_b = jnp.exp((c - row) * lg_b)
    kd_b = jnp.exp(row * lg_b)
    sd_f = jnp.exp(c * lg_f)
    sd_b = jnp.exp(c * lg_b)

    def chunk_step(ci, s, intra, qd, kd, sd, out_ref):
        r0 = pl.multiple_of(ci * c, c)
        qc = q_ref[pl.ds(r0, c), :]
        kc = k_ref[pl.ds(r0, c), :]
        vc = v_ref[pl.ds(r0, c), :]
        scores = lax.dot_general(qc, kc, (((1,), (1,)), ((), ())), preferred_element_type=F32) * intra
        inner = jnp.dot(scores.astype(BF16), vc, preferred_element_type=F32)
        cross = jnp.dot(qc, s.astype(BF16), preferred_element_type=F32) * qd
        out_ref[pl.ds(r0, c), :] = inner + cross
        kd_c = (kc.astype(F32) * kd).astype(BF16)
        kv = lax.dot_general(kd_c, vc, (((0,), (0,)), ((), ())), preferred_element_type=F32)
        return s * sd + kv

    def body(i, carry):
        s_f, s_b = carry
        s_f = chunk_step(i, s_f, intra_f, qd_f, kd_f, sd_f, of_ref)
        cb = jnp.where(i < n_ctx, n_ctx - 1 - i, n_chunks + n_ctx - 1 - i)
        s_b = chunk_step(cb, s_b, intra_b, qd_b, kd_b, sd_b, ob_ref)
        return s_f, s_b

    zero = jnp.zeros((c, c), F32)
    lax.fori_loop(0, n_chunks, body, (zero, zero))

    ng = ng_ref[...]

    def finish(i, _):
        r0 = pl.multiple_of(i * ROW_TILE, ROW_TILE)
        y = of_ref[pl.ds(r0, ROW_TILE), :] + ob_ref[pl.ds(r0, ROW_TILE), :]
        mu = jnp.mean(y, axis=-1, keepdims=True)
        yc = y - mu
        var = jnp.mean(yc * yc, axis=-1, keepdims=True)
        yn = yc * lax.rsqrt(var + NORM_EPS) * ng
        gate = rg_ref[pl.ds(r0, ROW_TILE), :]
        o_ref[pl.ds(r0, ROW_TILE), :] = (yn * (gate * jax.nn.sigmoid(gate))).astype(BF16)
        return 0

    lax.fori_loop(0, q_ref.shape[0] // ROW_TILE, finish, 0)


def _retention(rq, rk, rv, rg, decay_logit_lanes, norm_g):
    bsz, t_all, _ = rq.shape
    head = pl.BlockSpec((None, t_all, LANES), lambda b, h: (b, 0, h))
    return pl.pallas_call(
        _retention_kernel,
        grid=(bsz, RET_HEADS),
        in_specs=[head, head, head, head,
                  pl.BlockSpec((2 * RET_HEADS, LANES), lambda b, h: (0, 0)),
                  pl.BlockSpec((1, LANES), lambda b, h: (0, h))],
        out_specs=head,
        out_shape=jax.ShapeDtypeStruct((bsz, t_all, BRANCH_WIDTH), BF16),
        scratch_shapes=[pltpu.VMEM((t_all, LANES), F32), pltpu.VMEM((t_all, LANES), F32)],
        compiler_params=_cparams("parallel", "parallel"),
        name="bidirectional_retention",
    )(rq, rk, rv, rg, decay_logit_lanes, norm_g)


def _lru_kernel(lx_ref, lz_ref, cw_ref, cb_ref, w_ref, b_ref, lam_ref, o_ref,
                xp_ref, xs_ref, hf_ref, hb_ref):
    t_all = lx_ref.shape[0]
    ch = LRU_CHUNK
    n_chunks = t_all // ch
    pad = SUBLANES

    xp_ref[0:pad, :] = jnp.zeros((pad, LANES), F32)
    xp_ref[pad + t_all:pad + t_all + pad, :] = jnp.zeros((pad, LANES), F32)
    xp_ref[pad:pad + t_all, :] = lx_ref[...]
    cw = cw_ref[...]
    cb = cb_ref[...]
    for k in range(n_chunks):
        r0 = k * ch
        r = r0 + lax.broadcasted_iota(jnp.int32, (ch, LANES), 0)
        xm1 = xp_ref[pad + r0 - 1:pad + r0 - 1 + ch, :]
        x0 = xp_ref[pad + r0:pad + r0 + ch, :]
        xp1 = xp_ref[pad + r0 + 1:pad + r0 + 1 + ch, :]
        xp2 = xp_ref[pad + r0 + 2:pad + r0 + 2 + ch, :]
        if r0 <= CTX_LEN < r0 + ch:
            xm1 = jnp.where(r == CTX_LEN, 0.0, xm1)
        if r0 < CTX_LEN <= r0 + ch:
            xp1 = jnp.where(r == CTX_LEN - 1, 0.0, xp1)
            xp2 = jnp.where((r == CTX_LEN - 1) | (r == CTX_LEN - 2), 0.0, xp2)
        xs_ref[r0:r0 + ch, :] = cb + xm1 * cw[0:1] + x0 * cw[1:2] + xp1 * cw[2:3] + xp2 * cw[3:4]

    lam = lam_ref[...]
    sp = jnp.maximum(-lam, 0.0) + jnp.log1p(jnp.exp(-jnp.abs(lam)))
    sub = lax.broadcasted_iota(jnp.int32, (ch, LANES), 0) % SUBLANES
    n_groups = ch // SUBLANES

    def scan_chunk(ci, carry, d, out_ref):
        r0 = pl.multiple_of(ci * ch, ch)
        x = xs_ref[pl.ds(r0, ch), :]
        pre = jnp.dot(x.astype(BF16), w_ref[:, 2 * d * LANES:(2 * d + 2) * LANES],
                      preferred_element_type=F32) + b_ref[:, 2 * d * LANES:(2 * d + 2) * LANES]
        r_gate = jax.nn.sigmoid(pre[:, :LANES])
        i_gate = jax.nn.sigmoid(pre[:, LANES:])
        log_a = -LRU_C * r_gate * sp[d:d + 1]
        a = jnp.exp(log_a)
        th = jnp.tanh(log_a)
        u = jnp.sqrt(2.0 * th / (th - 1.0)) * (i_gate * x)
        for s in (1, 2, 4):
            if d == 0:
                a_sh = pltpu.roll(a, s, axis=0)
                u_sh = pltpu.roll(u, s, axis=0)
                ok = sub >= s
            else:
                a_sh = pltpu.roll(a, ch - s, axis=0)
                u_sh = pltpu.roll(u, ch - s, axis=0)
                ok = sub < SUBLANES - s
            u = jnp.where(ok, a * u_sh + u, u)
            a = jnp.where(ok, a * a_sh, a)
        groups = range(n_groups) if d == 0 else range(n_groups - 1, -1, -1)
        edge = SUBLANES - 1 if d == 0 else 0
        for gi in groups:
            lo = gi * SUBLANES
            hblk = u[lo:lo + SUBLANES] + a[lo:lo + SUBLANES] * carry
            out_ref[pl.ds(pl.multiple_of(r0 + lo, SUBLANES), SUBLANES), :] = hblk
            carry = hblk[edge:edge + 1]
        return carry

    n_ctx = CTX_LEN // ch

    def body(i, carry):
        c_f, c_b = carry
        c_f = scan_chunk(i, c_f, 0, hf_ref)
        cb_idx = jnp.where(i < n_ctx, n_ctx - 1 - i, n_chunks + n_ctx - 1 - i)
        c_b = scan_chunk(cb_idx, c_b, 1, hb_ref)
        return c_f, c_b

    zero = jnp.zeros((1, LANES), F32)
    lax.fori_loop(0, n_chunks, body, (zero, zero))

    def finish(i, _):
        r0 = pl.multiple_of(i * ch, ch)
        z = lz_ref[pl.ds(r0, ch), :]
        hsum = hf_ref[pl.ds(r0, ch), :] + hb_ref[pl.ds(r0, ch), :]
        o_ref[pl.ds(r0, ch), :] = (jax.nn.gelu(z) * hsum).astype(BF16)
        return 0

    lax.fori_loop(0, n_chunks, finish, 0)


def _lru(lx, lz, conv_w, conv_b, w_gates, b_gates, lam):
    bsz, t_all, width = lx.shape
    nb = width // LANES
    col = pl.BlockSpec((None, t_all, LANES), lambda b, j: (b, 0, j))
    return pl.pallas_call(
        _lru_kernel,
        grid=(bsz, nb),
        in_specs=[col, col,
                  pl.BlockSpec((4, LANES), lambda b, j: (0, j)),
                  pl.BlockSpec((1, LANES), lambda b, j: (0, j)),
                  pl.BlockSpec((None, LANES, 4 * LANES), lambda b, j: (j, 0, 0)),
                  pl.BlockSpec((None, 1, 4 * LANES), lambda b, j: (j, 0, 0)),
                  pl.BlockSpec((2, LANES), lambda b, j: (0, j))],
        out_specs=col,
        out_shape=jax.ShapeDtypeStruct((bsz, t_all, width), BF16),
        scratch_shapes=[pltpu.VMEM((t_all + 2 * SUBLANES, LANES), F32),
                        pltpu.VMEM((t_all, LANES), F32),
                        pltpu.VMEM((t_all, LANES), F32),
                        pltpu.VMEM((t_all, LANES), F32)],
        compiler_params=_cparams("parallel", "parallel"),
        name="bidirectional_rglru",
    )(lx, lz, conv_w, conv_b, w_gates, b_gates, lam)


def _attention_kernel(q_ref, k_ref, v_ref, o_ref):
    qt = pl.program_id(2)

    def attend(n_keys):
        k = k_ref[0:n_keys, :]
        v = v_ref[0:n_keys, :]
        for h in range(ATT_GROUP):
            sl = slice(h * ATT_HEAD_DIM, (h + 1) * ATT_HEAD_DIM)
            s = lax.dot_general(q_ref[:, sl], k, (((1,), (1,)), ((), ())), preferred_element_type=F32)
            m = jnp.max(s, axis=-1, keepdims=True)
            p = jnp.exp(s - m)
            l = jnp.sum(p, axis=-1, keepdims=True)
            o = jnp.dot(p.astype(BF16), v, preferred_element_type=F32)
            o_ref[:, sl] = (o / l).astype(BF16)

    @pl.when(qt < CTX_LEN // ROW_TILE)
    def _():
        attend(CTX_LEN)

    @pl.when(qt >= CTX_LEN // ROW_TILE)
    def _():
        attend(k_ref.shape[0])


def _attention(aq, ak, av):
    bsz, t_all, _ = aq.shape
    gw = ATT_GROUP * ATT_HEAD_DIM
    qo = pl.BlockSpec((None, ROW_TILE, gw), lambda b, g, t: (b, t, g))
    kv = pl.BlockSpec((None, None, t_all, ATT_HEAD_DIM), lambda b, g, t: (b, g, 0, 0))
    return pl.pallas_call(
        _attention_kernel,
        grid=(bsz, ATT_KV_HEADS, t_all // ROW_TILE),
        in_specs=[qo, kv, kv],
        out_specs=qo,
        out_shape=jax.ShapeDtypeStruct((bsz, t_all, BRANCH_WIDTH), BF16),
        compiler_params=_cparams("parallel", "parallel", "parallel"),
        name="prefix_gqa_attention",
    )(aq, ak, av)


def _merge_kernel(x_ref, mod_ref, g_ref, yr_ref, yl_ref, ya_ref, wg_ref, wb_ref, wo_ref, o_ref):
    x = x_ref[...]
    hb = _modulated_norm(x, g_ref[...], mod_ref[3:4, :], mod_ref[4:5, :]).astype(BF16)
    d = x.shape[-1]
    mix = jnp.zeros(x.shape, F32)
    for n, y_ref in enumerate((yr_ref, yl_ref, ya_ref)):
        gate = jax.nn.sigmoid(jnp.dot(hb, wg_ref[:, n * d:(n + 1) * d], preferred_element_type=F32))
        mix = mix + gate * jnp.dot(y_ref[...], wb_ref[n], preferred_element_type=F32)
    y = jnp.dot(mix.astype(BF16), wo_ref[...], preferred_element_type=F32)
    o_ref[...] = x + mod_ref[5:6, :] * y


def _merge(xs, mod, g, y_ret, y_lru, y_att, w_gate, w_branch, w_out):
    bsz, t_all, d = xs.shape
    tile = lambda n: pl.BlockSpec((None, ROW_TILE, n), lambda b, t: (b, t, 0))
    return pl.pallas_call(
        _merge_kernel,
        grid=(bsz, t_all // ROW_TILE),
        in_specs=[tile(d),
                  pl.BlockSpec((None, 3 * N_SUB, d), lambda b, t: (2 * b + _mod_row(t), 0, 0)),
                  _resident((1, d)),
                  tile(BRANCH_WIDTH), tile(BRANCH_WIDTH), tile(BRANCH_WIDTH),
                  _resident(w_gate.shape), _resident(w_branch.shape), _resident(w_out.shape)],
        out_specs=tile(d),
        out_shape=jax.ShapeDtypeStruct(xs.shape, F32),
        compiler_params=_cparams("parallel", "parallel"),
        name="gated_branch_merge",
    )(xs, mod, g, y_ret, y_lru, y_att, w_gate, w_branch, w_out)


def _rope_tables(seq, head_dim):
    half = head_dim // 2
    quarter = half // 2
    freqs = ROPE_THETA ** (-jnp.arange(0, half, 2, dtype=F32) / half)
    rows = jnp.repeat(jnp.arange(seq // GRID_W, dtype=F32), GRID_W)
    cols = jnp.tile(jnp.arange(GRID_W, dtype=F32), seq // GRID_W)
    ang_r = rows[:, None] * freqs[None]
    ang_c = cols[:, None] * freqs[None]
    ang = jnp.concatenate([ang_r, ang_r, ang_c, ang_c], axis=-1)
    sign = jnp.tile(jnp.concatenate([-jnp.ones((quarter,), F32), jnp.ones((quarter,), F32)]), 2)
    cos = jnp.tile(jnp.cos(ang), (1, LANES // head_dim))
    sin = jnp.tile(jnp.sin(ang) * sign[None], (1, LANES // head_dim))
    cos = jnp.concatenate([jnp.ones((CTX_LEN, LANES), F32), cos], axis=0)
    sin = jnp.concatenate([jnp.zeros((CTX_LEN, LANES), F32), sin], axis=0)
    return cos, sin


def _block_diag_lanes(w):
    per = LANES // LRU_BLOCK
    w = w.reshape(2, LRU_BLOCKS // per, per, LRU_BLOCK, LRU_BLOCK)
    eye = jnp.eye(per, dtype=w.dtype)
    dense = jnp.einsum('djpab,pq->djpaqb', w, eye).reshape(2, LRU_BLOCKS // per, LANES, LANES)
    return dense.transpose(1, 0, 2, 3)


def kernel(x, c, ctx, c_ctx, w_mod, b_mod, norm_g, ffn_w_in, ffn_w_out, w_in,
           ret_decay_logit, ret_norm_g, lru_conv_w, lru_conv_b, lru_w_a, lru_b_a,
           lru_w_x, lru_b_x, lru_lambda, attn_q_norm_g, attn_k_norm_g, w_branch,
           w_out, final_norm_g):
    bsz, seq, d = x.shape
    depth = w_mod.shape[0]
    xs = jnp.concatenate([ctx, x], axis=1)

    c8 = jnp.concatenate([c, c_ctx[None], jnp.zeros((SUBLANES - bsz - 1, d), F32)], axis=0)
    m = _modulation(c8, w_mod, b_mod).reshape(depth, SUBLANES, 3 * N_SUB, d)
    m_ctx = jnp.broadcast_to(m[:, bsz:bsz + 1], (depth, bsz, 3 * N_SUB, d))
    mods = jnp.stack([m_ctx, m[:, :bsz]], axis=2).reshape(depth, 2 * bsz, 3 * N_SUB, d)

    rope = _rope_tables(seq, RET_HEAD_DIM) + _rope_tables(seq, ATT_HEAD_DIM)
    nb = LRU_WIDTH // LANES

    for l in range(depth):
        last = l == depth - 1
        mod = mods[l]
        g = norm_g[l].reshape(N_SUB, 1, d)
        w1 = ffn_w_in[l].astype(BF16)
        w2 = ffn_w_out[l].astype(BF16)
        wi = w_in[l].astype(BF16)

        xs = _ffn(xs, mod, g[0], w1[0], w2[0], sub=0)

        rq, rk, rv, rg, lx, lz, aq, ak, av = _proj(
            xs, mod, g[1], wi[:, :MIX_COLS], rope,
            jnp.tile(attn_q_norm_g[l], 2)[None], jnp.tile(attn_k_norm_g[l], 2)[None])

        decay_lanes = jnp.broadcast_to(ret_decay_logit[l].reshape(2 * RET_HEADS, 1), (2 * RET_HEADS, LANES))
        y_ret = _retention(rq, rk, rv, rg, decay_lanes, ret_norm_g[l][None])

        wa = _block_diag_lanes(lru_w_a[l])
        wx = _block_diag_lanes(lru_w_x[l])
        w_gates = jnp.concatenate([wa[:, 0], wx[:, 0], wa[:, 1], wx[:, 1]], axis=-1).astype(BF16)
        ba = lru_b_a[l].reshape(2, nb, LANES)
        bx = lru_b_x[l].reshape(2, nb, LANES)
        b_gates = jnp.concatenate([ba[0], bx[0], ba[1], bx[1]], axis=-1)[:, None, :]
        y_lru = _lru(lx, lz, lru_conv_w[l], lru_conv_b[l][None], w_gates, b_gates, lru_lambda[l])

        y_att = _attention(aq, ak, av)

        xs = _merge(xs, mod, g[1], y_ret, y_lru, y_att, wi[:, MIX_COLS:],
                    w_branch[l].astype(BF16), w_out[l].astype(BF16))

        if last:
            xs = _ffn(xs, mod, g[2], w1[1], w2[1], sub=2, tile_off=CTX_LEN // ROW_TILE,
                      final_g=final_norm_g[None])
        else:
            xs = _ffn(xs, mod, g[2], w1[1], w2[1], sub=2)
    return xs
```

```python
import functools

import jax
import jax.numpy as jnp
from jax import lax
from jax.experimental import pallas as pl
from jax.experimental.pallas import tpu as pltpu

D_MODEL = 1024
GRID_W = 64
CTX_LEN = 256
N_BRANCH = 3
BRANCH_WIDTH = 512
RET_HEADS = 4
RET_HEAD_DIM = 128
RET_CHUNK = 128
LRU_WIDTH = 512
LRU_BLOCKS = 8
LRU_BLOCK = 64
LRU_C = 8.0
ATT_HEAD_DIM = 64
ATT_Q_HEADS = 8
ATT_KV_HEADS = 2
ATT_GROUP = 4
ROPE_THETA = 10000.0
D_FF = 2816
MACARON_WEIGHT = 0.5
N_SUB = 3
NORM_EPS = 1e-6
LOG2_E = 1.4426950408889634
MIX_COLS = 4 * BRANCH_WIDTH + 2 * LRU_WIDTH + BRANCH_WIDTH + 2 * ATT_HEAD_DIM * ATT_KV_HEADS

LANES = 128
SUBLANES = 8
VMEM_LIMIT = 56 * 1024 * 1024

ROW_TILE = 256
FFN_CHUNK = 256
LRU_CHUNK = 256
RET_HEADS_PER_STEP = 2

BF16 = jnp.bfloat16
F32 = jnp.float32


def _cparams(*sem):
    return pltpu.CompilerParams(dimension_semantics=sem, vmem_limit_bytes=VMEM_LIMIT)


def _pinned(shape, *lead):
    tail = tuple(shape[len(lead):])
    index = tuple(lead) + (0,) * len(tail)
    return pl.BlockSpec((None,) * len(lead) + tail, lambda *_: index, pipeline_mode=pl.Buffered(1))


def _mod_spec(layer, d, tile_off=0):
    n_ctx_tiles = CTX_LEN // ROW_TILE
    return pl.BlockSpec(
        (None, None, 3 * N_SUB, d),
        lambda b, t: (layer, 2 * b + jnp.minimum((t + tile_off) // n_ctx_tiles, 1), 0, 0))


def _modulated_norm(x, g, shift, scale):
    ms = jnp.mean(x * x, axis=-1, keepdims=True)
    return x * lax.rsqrt(ms + NORM_EPS) * g * (1.0 + scale) + shift


def _mod_kernel(c_ref, w_ref, b_ref, o_ref):
    c = c_ref[...]
    s = (c * jax.nn.sigmoid(c)).astype(BF16)
    o_ref[...] = jnp.dot(s, w_ref[...].astype(BF16), preferred_element_type=F32) + b_ref[...]


def _modulation(c8, w_mod, b_mod):
    depth, d, n = w_mod.shape
    tn = 1024
    return pl.pallas_call(
        _mod_kernel,
        grid=(depth, n // tn),
        in_specs=[
            pl.BlockSpec((SUBLANES, d), lambda l, j: (0, 0)),
            pl.BlockSpec((None, d, tn), lambda l, j: (l, 0, j)),
            pl.BlockSpec((None, 1, tn), lambda l, j: (l, 0, j)),
        ],
        out_specs=pl.BlockSpec((None, SUBLANES, tn), lambda l, j: (l, 0, j)),
        out_shape=jax.ShapeDtypeStruct((depth, SUBLANES, n), F32),
        compiler_params=_cparams("parallel", "parallel"),
        name="adaln_modulation",
    )(c8, w_mod, b_mod.reshape(depth, 1, n))


def _ffn_kernel(x_ref, mod_ref, g_ref, w1_ref, w2_ref, *rest, sub, final_norm):
    if final_norm:
        fg_ref, o_ref = rest
    else:
        (o_ref,) = rest
    x = x_ref[...]
    shift = mod_ref[3 * sub + 0:3 * sub + 1, :]
    scale = mod_ref[3 * sub + 1:3 * sub + 2, :]
    gate = mod_ref[3 * sub + 2:3 * sub + 3, :]
    hb = _modulated_norm(x, g_ref[...], shift, scale).astype(BF16)
    acc = jnp.zeros(x.shape, F32)
    for c in range(D_FF // FFN_CHUNK):
        lo = c * FFN_CHUNK
        a = jnp.dot(hb, w1_ref[:, lo:lo + FFN_CHUNK], preferred_element_type=F32)
        b = jnp.dot(hb, w1_ref[:, D_FF + lo:D_FF + lo + FFN_CHUNK], preferred_element_type=F32)
        act = (a * jax.nn.sigmoid(a) * b).astype(BF16)
        acc = acc + jnp.dot(act, w2_ref[lo:lo + FFN_CHUNK, :], preferred_element_type=F32)
    y = x + (MACARON_WEIGHT * gate) * acc
    if final_norm:
        ms = jnp.mean(y * y, axis=-1, keepdims=True)
        y = y * lax.rsqrt(ms + NORM_EPS) * fg_ref[...]
    o_ref[...] = y


def _ffn(xs, mods, norm_g, w1, w2, *, layer, which, tile_off=0, final_g=None):
    bsz, t_all, d = xs.shape
    sub = 2 * which
    nt = t_all // ROW_TILE - tile_off
    in_specs = [
        pl.BlockSpec((None, ROW_TILE, d), lambda b, t: (b, t + tile_off, 0)),
        _mod_spec(layer, d, tile_off),
        _pinned(norm_g.shape, layer, sub),
        _pinned(w1.shape, layer, which),
        _pinned(w2.shape, layer, which),
    ]
    args = [xs, mods, norm_g, w1, w2]
    if final_g is not None:
        in_specs.append(_pinned(final_g.shape))
        args.append(final_g)
    return pl.pallas_call(
        functools.partial(_ffn_kernel, sub=sub, final_norm=final_g is not None),
        grid=(bsz, nt),
        in_specs=in_specs,
        out_specs=pl.BlockSpec((None, ROW_TILE, d), lambda b, t: (b, t, 0)),
        out_shape=jax.ShapeDtypeStruct((bsz, nt * ROW_TILE, d), F32),
        compiler_params=_cparams("parallel", "parallel"),
        name="swiglu_half_step",
    )(*args)


def _rope_block(x, cos, sin_signed, half):
    lane = lax.broadcasted_iota(jnp.int32, x.shape, 1)
    up = pltpu.roll(x, LANES - half, axis=1)
    down = pltpu.roll(x, half, axis=1)
    partner = jnp.where((lane & half) == 0, up, down)
    return x * cos + partner * sin_signed


def _head_pair_rms(x, g):
    lane = lax.broadcasted_iota(jnp.int32, x.shape, 1)
    first = lane < ATT_HEAD_DIM
    sq = x * x
    s0 = jnp.sum(jnp.where(first, sq, 0.0), axis=-1, keepdims=True)
    s1 = jnp.sum(jnp.where(first, 0.0, sq), axis=-1, keepdims=True)
    ms = jnp.where(first, s0, s1) * (1.0 / ATT_HEAD_DIM)
    return x * lax.rsqrt(ms + NORM_EPS) * g


def _proj_kernel(x_ref, mod_ref, g_ref, w_ref, rcos_ref, rsin_ref, acos_ref, asin_ref,
                 qg_ref, kg_ref,
                 rq_ref, rk_ref, rv_ref, rg_ref, lx_ref, lz_ref, aq_ref, ak_ref, avt_ref):
    x = x_ref[...]
    hb = _modulated_norm(x, g_ref[...], mod_ref[3:4, :], mod_ref[4:5, :]).astype(BF16)
    w = BRANCH_WIDTH

    def cols(lo, n):
        return jnp.dot(hb, w_ref[:, lo:lo + n], preferred_element_type=F32)

    rcos, rsin = rcos_ref[...], rsin_ref[...]
    acos, asin = acos_ref[...], asin_ref[...]
    k_scale = RET_HEAD_DIM ** -0.5
    zq = cols(0, w)
    zk = cols(w, w)
    for h in range(RET_HEADS):
        sl = slice(h * LANES, (h + 1) * LANES)
        rq_ref[:, sl] = _rope_block(zq[:, sl], rcos, rsin, RET_HEAD_DIM // 4).astype(BF16)
        rk_ref[:, sl] = (_rope_block(zk[:, sl], rcos, rsin, RET_HEAD_DIM // 4) * k_scale).astype(BF16)
    rv_ref[...] = cols(2 * w, w).astype(BF16)
    rg_ref[...] = cols(3 * w, w)
    lx_ref[...] = cols(4 * w, w)
    lz_ref[...] = cols(5 * w, w)
    za = cols(6 * w, w)
    q_scale = ATT_HEAD_DIM ** -0.5 * LOG2_E
    for p in range(ATT_Q_HEADS // 2):
        sl = slice(p * LANES, (p + 1) * LANES)
        qn = _head_pair_rms(za[:, sl], qg_ref[...])
        aq_ref[:, sl] = (_rope_block(qn, acos, asin, ATT_HEAD_DIM // 4) * q_scale).astype(BF16)
    zkv = cols(7 * w, 2 * LANES)
    kn = _rope_block(_head_pair_rms(zkv[:, :LANES], kg_ref[...]), acos, asin, ATT_HEAD_DIM // 4)
    for j in range(ATT_KV_HEADS):
        ak_ref[j] = kn[:, j * ATT_HEAD_DIM:(j + 1) * ATT_HEAD_DIM].astype(BF16)
    vt = zkv[:, LANES:].T
    for j in range(ATT_KV_HEADS):
        avt_ref[j] = vt[j * ATT_HEAD_DIM:(j + 1) * ATT_HEAD_DIM, :].astype(BF16)


def _proj(xs, mods, norm_g, w_mix, rope, qg, kg, *, layer):
    bsz, t_all, d = xs.shape
    nt = t_all // ROW_TILE
    w = BRANCH_WIDTH
    tile = lambda n: pl.BlockSpec((None, ROW_TILE, n), lambda b, t: (b, t, 0))
    table = pl.BlockSpec((ROW_TILE, LANES), lambda b, t: (t, 0))
    kv_spec = pl.BlockSpec((None, ATT_KV_HEADS, ROW_TILE, ATT_HEAD_DIM), lambda b, t: (b, 0, t, 0))
    sds = lambda n, dt: jax.ShapeDtypeStruct((bsz, t_all, n), dt)
    kv_sds = jax.ShapeDtypeStruct((bsz, ATT_KV_HEADS, t_all, ATT_HEAD_DIM), BF16)
    vt_spec = pl.BlockSpec((None, ATT_KV_HEADS, ATT_HEAD_DIM, ROW_TILE), lambda b, t: (b, 0, 0, t))
    vt_sds = jax.ShapeDtypeStruct((bsz, ATT_KV_HEADS, ATT_HEAD_DIM, t_all), BF16)
    return pl.pallas_call(
        _proj_kernel,
        grid=(bsz, nt),
        in_specs=[
            tile(d),
            _mod_spec(layer, d),
            _pinned(norm_g.shape, layer, 1),
            _pinned(w_mix.shape, layer),
            table, table, table, table,
            _pinned(qg.shape, layer), _pinned(kg.shape, layer),
        ],
        out_specs=[tile(w), tile(w), tile(w), tile(w), tile(w), tile(w), tile(w), kv_spec, vt_spec],
        out_shape=[sds(w, BF16), sds(w, BF16), sds(w, BF16), sds(w, F32), sds(w, F32), sds(w, F32),
                   sds(w, BF16), kv_sds, vt_sds],
        compiler_params=_cparams("parallel", "parallel"),
        name="mixer_input_projection",
    )(xs, mods, norm_g, w_mix, *rope, qg, kg)


def _log_sigmoid(x):
    return jnp.minimum(x, 0.0) - jnp.log1p(jnp.exp(-jnp.abs(x)))


def _retention_kernel(q_ref, k_ref, v_ref, rg_ref, dl_ref, ng_ref, o_ref, s_ref):
    hb = pl.program_id(1)
    c = RET_CHUNK
    n_chunks = q_ref.shape[0] // c
    n_ctx = CTX_LEN // c
    row = lax.broadcasted_iota(jnp.int32, (c, c), 0).astype(F32)
    col = lax.broadcasted_iota(jnp.int32, (c, c), 1).astype(F32)
    diff = row - col

    consts = []
    for j in range(RET_HEADS_PER_STEP):
        h = hb * RET_HEADS_PER_STEP + j
        lg_f = _log_sigmoid(dl_ref[pl.ds(h, 1), :])
        lg_b = _log_sigmoid(dl_ref[pl.ds(RET_HEADS + h, 1), :])
        consts.append(dict(
            intra=(jnp.where(diff >= 0, jnp.exp(jnp.maximum(diff, 0.0) * lg_f), 0.0)
                   + jnp.where(diff <= 0, jnp.exp(jnp.maximum(-diff, 0.0) * lg_b), 0.0)),
            qd_f=jnp.exp((row + 1.0) * lg_f), kd_f=jnp.exp((c - 1.0 - row) * lg_f),
            qd_b=jnp.exp((c - row) * lg_b), kd_b=jnp.exp(row * lg_b),
            sd_f=jnp.exp(c * lg_f), sd_b=jnp.exp(c * lg_b)))

    def state_step(j, ci, s, kd, sd, lane0):
        r0 = pl.multiple_of(ci * c, c)
        sl = slice(j * LANES, (j + 1) * LANES)
        s_ref[j, ci, :, lane0:lane0 + LANES] = s.astype(BF16)
        kd_c = (k_ref[pl.ds(r0, c), sl].astype(F32) * kd).astype(BF16)
        kv = lax.dot_general(kd_c, v_ref[pl.ds(r0, c), sl], (((0,), (0,)), ((), ())),
                             preferred_element_type=F32)
        return s * sd + kv

    def pass1(i, carry):
        cb = jnp.where(i < n_ctx, n_ctx - 1 - i, n_chunks + n_ctx - 1 - i)
        out = []
        for j in range(RET_HEADS_PER_STEP):
            k = consts[j]
            s_f, s_b = carry[j]
            out.append((state_step(j, i, s_f, k["kd_f"], k["sd_f"], 0),
                        state_step(j, cb, s_b, k["kd_b"], k["sd_b"], LANES)))
        return tuple(out)

    zero = jnp.zeros((c, c), F32)
    lax.fori_loop(0, n_chunks, pass1, tuple((zero, zero) for _ in range(RET_HEADS_PER_STEP)), unroll=2)

    def pass2(ci, _):
        r0 = pl.multiple_of(ci * c, c)
        for j in range(RET_HEADS_PER_STEP):
            k = consts[j]
            sl = slice(j * LANES, (j + 1) * LANES)
            qc = q_ref[pl.ds(r0, c), sl]
            kc = k_ref[pl.ds(r0, c), sl]
            vc = v_ref[pl.ds(r0, c), sl]
            scores = lax.dot_general(qc, kc, (((1,), (1,)), ((), ())), preferred_element_type=F32) * k["intra"]
            inner = jnp.dot(scores.astype(BF16), vc, preferred_element_type=F32)
            cross = jnp.dot(qc, s_ref[j, ci], preferred_element_type=F32)
            y = inner + cross[:, :LANES] * k["qd_f"] + cross[:, LANES:] * k["qd_b"]
            mu = jnp.mean(y, axis=-1, keepdims=True)
            yc = y - mu
            var = jnp.mean(yc * yc, axis=-1, keepdims=True)
            yn = yc * lax.rsqrt(var + NORM_EPS) * ng_ref[:, sl]
            gate = rg_ref[pl.ds(r0, c), sl]
            o_ref[pl.ds(r0, c), sl] = (yn * (gate * jax.nn.sigmoid(gate))).astype(BF16)
        return 0

    lax.fori_loop(0, n_chunks, pass2, 0, unroll=2)


def _retention(rq, rk, rv, rg, decay_lanes, norm_g, *, layer):
    bsz, t_all, _ = rq.shape
    w = RET_HEADS_PER_STEP * LANES
    head = pl.BlockSpec((None, t_all, w), lambda b, h: (b, 0, h))
    return pl.pallas_call(
        _retention_kernel,
        grid=(bsz, RET_HEADS // RET_HEADS_PER_STEP),
        in_specs=[head, head, head, head,
                  pl.BlockSpec((None, 2 * RET_HEADS, LANES), lambda b, h: (layer, 0, 0)),
                  pl.BlockSpec((None, 1, w), lambda b, h: (layer, 0, h))],
        out_specs=head,
        out_shape=jax.ShapeDtypeStruct((bsz, t_all, BRANCH_WIDTH), BF16),
        scratch_shapes=[pltpu.VMEM((RET_HEADS_PER_STEP, t_all // RET_CHUNK, RET_HEAD_DIM, 2 * RET_HEAD_DIM), BF16)],
        compiler_params=_cparams("parallel", "parallel"),
        name="bidirectional_retention",
    )(rq, rk, rv, rg, decay_lanes, norm_g)


def _lru_kernel(lx_ref, lz_ref, cw_ref, cb_ref, w_ref, b_ref, lam_ref, o_ref,
                xp_ref, xs_ref, hf_ref, hb_ref):
    t_all = lx_ref.shape[0]
    ch = LRU_CHUNK
    n_chunks = t_all // ch
    pad = SUBLANES

    xp_ref[0:pad, :] = jnp.zeros((pad, LANES), F32)
    xp_ref[pad + t_all:pad + t_all + pad, :] = jnp.zeros((pad, LANES), F32)
    xp_ref[pad:pad + t_all, :] = lx_ref[...]
    cw = cw_ref[...]
    cb = cb_ref[...]
    for k in range(n_chunks):
        r0 = k * ch
        r = r0 + lax.broadcasted_iota(jnp.int32, (ch, LANES), 0)
        xm1 = xp_ref[pad + r0 - 1:pad + r0 - 1 + ch, :]
        x0 = xp_ref[pad + r0:pad + r0 + ch, :]
        xp1 = xp_ref[pad + r0 + 1:pad + r0 + 1 + ch, :]
        xp2 = xp_ref[pad + r0 + 2:pad + r0 + 2 + ch, :]
        if r0 <= CTX_LEN < r0 + ch:
            xm1 = jnp.where(r == CTX_LEN, 0.0, xm1)
        if r0 < CTX_LEN <= r0 + ch:
            xp1 = jnp.where(r == CTX_LEN - 1, 0.0, xp1)
            xp2 = jnp.where((r == CTX_LEN - 1) | (r == CTX_LEN - 2), 0.0, xp2)
        xs_ref[r0:r0 + ch, :] = cb + xm1 * cw[0:1] + x0 * cw[1:2] + xp1 * cw[2:3] + xp2 * cw[3:4]

    lam = lam_ref[...]
    sp = jnp.maximum(-lam, 0.0) + jnp.log1p(jnp.exp(-jnp.abs(lam)))
    sub = lax.broadcasted_iota(jnp.int32, (ch, LANES), 0) % SUBLANES
    n_groups = ch // SUBLANES

    def scan_chunk(ci, carry, d, out_ref):
        r0 = pl.multiple_of(ci * ch, ch)
        x = xs_ref[pl.ds(r0, ch), :]
        pre = jnp.dot(x.astype(BF16), w_ref[:, 2 * d * LANES:(2 * d + 2) * LANES],
                      preferred_element_type=F32) + b_ref[:, 2 * d * LANES:(2 * d + 2) * LANES]
        r_gate = jax.nn.sigmoid(pre[:, :LANES])
        i_gate = jax.nn.sigmoid(pre[:, LANES:])
        log_a = -LRU_C * r_gate * sp[d:d + 1]
        a = jnp.exp(log_a)
        th = jnp.tanh(log_a)
        u = jnp.sqrt(2.0 * th / (th - 1.0)) * (i_gate * x)
        for s in (1, 2, 4):
            if d == 0:
                a_sh = pltpu.roll(a, s, axis=0)
                u_sh = pltpu.roll(u, s, axis=0)
                ok = sub >= s
            else:
                a_sh = pltpu.roll(a, ch - s, axis=0)
                u_sh = pltpu.roll(u, ch - s, axis=0)
                ok = sub < SUBLANES - s
            u = jnp.where(ok, a * u_sh + u, u)
            a = jnp.where(ok, a * a_sh, a)
        groups = range(n_groups) if d == 0 else range(n_groups - 1, -1, -1)
        edge = SUBLANES - 1 if d == 0 else 0
        for gi in groups:
            lo = gi * SUBLANES
            hblk = u[lo:lo + SUBLANES] + a[lo:lo + SUBLANES] * carry
            out_ref[pl.ds(pl.multiple_of(r0 + lo, SUBLANES), SUBLANES), :] = hblk
            carry = hblk[edge:edge + 1]
        return carry

    n_ctx = CTX_LEN // ch

    def body(i, carry):
        c_f, c_b = carry
        c_f = scan_chunk(i, c_f, 0, hf_ref)
        cb_idx = jnp.where(i < n_ctx, n_ctx - 1 - i, n_chunks + n_ctx - 1 - i)
        c_b = scan_chunk(cb_idx, c_b, 1, hb_ref)
        return c_f, c_b

    zero = jnp.zeros((1, LANES), F32)
    lax.fori_loop(0, n_chunks, body, (zero, zero))

    def finish(i, _):
        r0 = pl.multiple_of(i * ch, ch)
        z = lz_ref[pl.ds(r0, ch), :]
        hsum = hf_ref[pl.ds(r0, ch), :] + hb_ref[pl.ds(r0, ch), :]
        o_ref[pl.ds(r0, ch), :] = (jax.nn.gelu(z) * hsum).astype(BF16)
        return 0

    lax.fori_loop(0, n_chunks, finish, 0)


def _lru(lx, lz, conv_w, conv_b, w_gates, b_gates, lam, *, layer):
    bsz, t_all, width = lx.shape
    nb = width // LANES
    col = pl.BlockSpec((None, t_all, LANES), lambda b, j: (b, 0, j))
    return pl.pallas_call(
        _lru_kernel,
        grid=(bsz, nb),
        in_specs=[col, col,
                  pl.BlockSpec((None, 4, LANES), lambda b, j: (layer, 0, j)),
                  pl.BlockSpec((None, 1, LANES), lambda b, j: (layer, 0, j)),
                  pl.BlockSpec((None, None, LANES, 4 * LANES), lambda b, j: (layer, j, 0, 0)),
                  pl.BlockSpec((None, None, 1, 4 * LANES), lambda b, j: (layer, j, 0, 0)),
                  pl.BlockSpec((None, 2, LANES), lambda b, j: (layer, 0, j))],
        out_specs=col,
        out_shape=jax.ShapeDtypeStruct((bsz, t_all, width), BF16),
        scratch_shapes=[pltpu.VMEM((t_all + 2 * SUBLANES, LANES), F32),
                        pltpu.VMEM((t_all, LANES), F32),
                        pltpu.VMEM((t_all, LANES), F32),
                        pltpu.VMEM((t_all, LANES), F32)],
        compiler_params=_cparams("parallel", "parallel"),
        name="bidirectional_rglru",
    )(lx, lz, conv_w, conv_b, w_gates, b_gates, lam)


def _attention_kernel(q_ref, k_ref, vt_ref, o_ref):
    qt = pl.program_id(2)
    hd = ATT_HEAD_DIM
    rows = q_ref.shape[0]

    def attend(n_keys):
        k = k_ref[0:n_keys, :]
        vt = jnp.concatenate([vt_ref[:, 0:n_keys], jnp.ones((2 * SUBLANES, n_keys), BF16)], axis=0)
        outs = []
        scores = []
        for pair in range(ATT_GROUP // 2):
            q2 = jnp.concatenate([q_ref[:, (2 * pair) * hd:(2 * pair + 1) * hd],
                                  q_ref[:, (2 * pair + 1) * hd:(2 * pair + 2) * hd]], axis=0)
            scores.append(lax.dot_general(k, q2, (((1,), (1,)), ((), ())),
                                          preferred_element_type=F32))
        for st in scores:
            m = jnp.max(st, axis=0, keepdims=True)
            p = jnp.exp2(st - m).astype(BF16)
            ot = jnp.dot(vt, p, preferred_element_type=F32)
            ot = ot[:hd] / ot[hd:hd + 1]
            outs += [ot[:, :rows].T, ot[:, rows:].T]
        o_ref[...] = jnp.concatenate(outs, axis=1).astype(BF16)

    @pl.when(qt < CTX_LEN // ROW_TILE)
    def _():
        attend(CTX_LEN)

    @pl.when(qt >= CTX_LEN // ROW_TILE)
    def _():
        attend(k_ref.shape[0])


def _attention(aq, ak, avt):
    bsz, t_all, _ = aq.shape
    gw = ATT_GROUP * ATT_HEAD_DIM
    qo = pl.BlockSpec((None, ROW_TILE, gw), lambda b, g, t: (b, t, g))
    k_spec = pl.BlockSpec((None, None, t_all, ATT_HEAD_DIM), lambda b, g, t: (b, g, 0, 0))
    vt_spec = pl.BlockSpec((None, None, ATT_HEAD_DIM, t_all), lambda b, g, t: (b, g, 0, 0))
    return pl.pallas_call(
        _attention_kernel,
        grid=(bsz, ATT_KV_HEADS, t_all // ROW_TILE),
        in_specs=[qo, k_spec, vt_spec],
        out_specs=qo,
        out_shape=jax.ShapeDtypeStruct((bsz, t_all, BRANCH_WIDTH), BF16),
        compiler_params=_cparams("parallel", "parallel", "parallel"),
        name="prefix_gqa_attention",
    )(aq, ak, avt)


def _merge_kernel(x_ref, mod_ref, g_ref, yr_ref, yl_ref, ya_ref, wg_ref, wb_ref, wo_ref, o_ref):
    x = x_ref[...]
    hb = _modulated_norm(x, g_ref[...], mod_ref[3:4, :], mod_ref[4:5, :]).astype(BF16)
    d = x.shape[-1]
    mix = jnp.zeros(x.shape, F32)
    for n, y_ref in enumerate((yr_ref, yl_ref, ya_ref)):
        gate = jax.nn.sigmoid(jnp.dot(hb, wg_ref[:, n * d:(n + 1) * d], preferred_element_type=F32))
        mix = mix + gate * jnp.dot(y_ref[...], wb_ref[n], preferred_element_type=F32)
    y = jnp.dot(mix.astype(BF16), wo_ref[...], preferred_element_type=F32)
    o_ref[...] = x + mod_ref[5:6, :] * y


def _merge(xs, mods, norm_g, y_ret, y_lru, y_att, w_gate, w_branch, w_out, *, layer):
    bsz, t_all, d = xs.shape
    tile = lambda n: pl.BlockSpec((None, ROW_TILE, n), lambda b, t: (b, t, 0))
    return pl.pallas_call(
        _merge_kernel,
        grid=(bsz, t_all // ROW_TILE),
        in_specs=[tile(d),
                  _mod_spec(layer, d),
                  _pinned(norm_g.shape, layer, 1),
                  tile(BRANCH_WIDTH), tile(BRANCH_WIDTH), tile(BRANCH_WIDTH),
                  _pinned(w_gate.shape, layer), _pinned(w_branch.shape, layer), _pinned(w_out.shape, layer)],
        out_specs=tile(d),
        out_shape=jax.ShapeDtypeStruct(xs.shape, F32),
        compiler_params=_cparams("parallel", "parallel"),
        name="gated_branch_merge",
    )(xs, mods, norm_g, y_ret, y_lru, y_att, w_gate, w_branch, w_out)


def _rope_tables(seq, head_dim):
    half = head_dim // 2
    quarter = half // 2
    freqs = ROPE_THETA ** (-jnp.arange(0, half, 2, dtype=F32) / half)
    rows = jnp.repeat(jnp.arange(seq // GRID_W, dtype=F32), GRID_W)
    cols = jnp.tile(jnp.arange(GRID_W, dtype=F32), seq // GRID_W)
    ang_r = rows[:, None] * freqs[None]
    ang_c = cols[:, None] * freqs[None]
    ang = jnp.concatenate([ang_r, ang_r, ang_c, ang_c], axis=-1)
    sign = jnp.tile(jnp.concatenate([-jnp.ones((quarter,), F32), jnp.ones((quarter,), F32)]), 2)
    cos = jnp.tile(jnp.cos(ang), (1, LANES // head_dim))
    sin = jnp.tile(jnp.sin(ang) * sign[None], (1, LANES // head_dim))
    cos = jnp.concatenate([jnp.ones((CTX_LEN, LANES), F32), cos], axis=0)
    sin = jnp.concatenate([jnp.zeros((CTX_LEN, LANES), F32), sin], axis=0)
    return cos, sin


def _block_diag_lanes(w):
    depth = w.shape[0]
    per = LANES // LRU_BLOCK
    w = w.reshape(depth, 2, LRU_BLOCKS // per, per, LRU_BLOCK, LRU_BLOCK)
    eye = jnp.eye(per, dtype=w.dtype)
    dense = jnp.einsum('ldjpab,pq->ldjpaqb', w, eye).reshape(depth, 2, LRU_BLOCKS // per, LANES, LANES)
    return dense.transpose(0, 2, 1, 3, 4)


def kernel(x, c, ctx, c_ctx, w_mod, b_mod, norm_g, ffn_w_in, ffn_w_out, w_in,
           ret_decay_logit, ret_norm_g, lru_conv_w, lru_conv_b, lru_w_a, lru_b_a,
           lru_w_x, lru_b_x, lru_lambda, attn_q_norm_g, attn_k_norm_g, w_branch,
           w_out, final_norm_g):
    bsz, seq, d = x.shape
    depth = w_mod.shape[0]
    nb = LRU_WIDTH // LANES
    xs = jnp.concatenate([ctx, x], axis=1)

    c8 = jnp.concatenate([c, c_ctx[None], jnp.zeros((SUBLANES - bsz - 1, d), F32)], axis=0)
    m = _modulation(c8, w_mod, b_mod).reshape(depth, SUBLANES, 3 * N_SUB, d)
    m_ctx = jnp.broadcast_to(m[:, bsz:bsz + 1], (depth, bsz, 3 * N_SUB, d))
    mods = jnp.stack([m_ctx, m[:, :bsz]], axis=2).reshape(depth, 2 * bsz, 3 * N_SUB, d)

    g_all = norm_g.reshape(depth, N_SUB, 1, d)
    w1 = ffn_w_in.astype(BF16)
    w2 = ffn_w_out.astype(BF16)
    w_mix = w_in[:, :, :MIX_COLS].astype(BF16)
    w_gate = w_in[:, :, MIX_COLS:].astype(BF16)
    wb = w_branch.astype(BF16)
    wo = w_out.astype(BF16)
    rope = _rope_tables(seq, RET_HEAD_DIM) + _rope_tables(seq, ATT_HEAD_DIM)
    qg = jnp.tile(attn_q_norm_g, (1, 2))[:, None, :]
    kg = jnp.tile(attn_k_norm_g, (1, 2))[:, None, :]
    decay_lanes = jnp.broadcast_to(ret_decay_logit.reshape(depth, 2 * RET_HEADS, 1), (depth, 2 * RET_HEADS, LANES))
    ret_g = ret_norm_g[:, None, :]
    wa = _block_diag_lanes(lru_w_a)
    wx = _block_diag_lanes(lru_w_x)
    w_gates = jnp.concatenate([wa[:, :, 0], wx[:, :, 0], wa[:, :, 1], wx[:, :, 1]], axis=-1).astype(BF16)
    ba = lru_b_a.reshape(depth, 2, nb, LANES)
    bx = lru_b_x.reshape(depth, 2, nb, LANES)
    b_gates = jnp.concatenate([ba[:, 0], bx[:, 0], ba[:, 1], bx[:, 1]], axis=-1)[:, :, None, :]
    conv_b = lru_conv_b[:, None, :]
    final_g = final_norm_g[None]

    for l in range(depth):
        last = l == depth - 1
        xs = _ffn(xs, mods, g_all, w1, w2, layer=l, which=0)
        rq, rk, rv, rg, lx, lz, aq, ak, av = _proj(xs, mods, g_all, w_mix, rope, qg, kg, layer=l)
        y_ret = _retention(rq, rk, rv, rg, decay_lanes, ret_g, layer=l)
        y_lru = _lru(lx, lz, lru_conv_w, conv_b, w_gates, b_gates, lru_lambda, layer=l)
        y_att = _attention(aq, ak, av)
        xs = _merge(xs, mods, g_all, y_ret, y_lru, y_att, w_gate, wb, wo, layer=l)
        if last:
            xs = _ffn(xs, mods, g_all, w1, w2, layer=l, which=1, tile_off=CTX_LEN // ROW_TILE,
                      final_g=final_g)
        else:
            xs = _ffn(xs, mods, g_all, w1, w2, layer=l, which=1)
    return xs
```

```python
import functools

import jax
import jax.numpy as jnp
from jax import lax
from jax.experimental import pallas as pl
from jax.experimental.pallas import tpu as pltpu

D_MODEL = 1024
GRID_W = 64
CTX_LEN = 256
N_BRANCH = 3
BRANCH_WIDTH = 512
RET_HEADS = 4
RET_HEAD_DIM = 128
RET_CHUNK = 128
LRU_WIDTH = 512
LRU_BLOCKS = 8
LRU_BLOCK = 64
LRU_C = 8.0
ATT_HEAD_DIM = 64
ATT_Q_HEADS = 8
ATT_KV_HEADS = 2
ATT_GROUP = 4
ROPE_THETA = 10000.0
D_FF = 2816
MACARON_WEIGHT = 0.5
N_SUB = 3
NORM_EPS = 1e-6
LOG2_E = 1.4426950408889634
MIX_COLS = 4 * BRANCH_WIDTH + 2 * LRU_WIDTH + BRANCH_WIDTH + 2 * ATT_HEAD_DIM * ATT_KV_HEADS

LANES = 128
SUBLANES = 8
VMEM_LIMIT = 56 * 1024 * 1024

ROW_TILE = 256
FFN_CHUNK = 256
LRU_CHUNK = 256
RET_HEADS_PER_STEP = 2

BF16 = jnp.bfloat16
F32 = jnp.float32


def _cparams(*sem):
    return pltpu.CompilerParams(dimension_semantics=sem, vmem_limit_bytes=VMEM_LIMIT)


def _pinned(shape, *lead):
    tail = tuple(shape[len(lead):])
    index = tuple(lead) + (0,) * len(tail)
    return pl.BlockSpec((None,) * len(lead) + tail, lambda *_: index, pipeline_mode=pl.Buffered(1))


def _mod_spec(layer, d, tile_off=0):
    n_ctx_tiles = CTX_LEN // ROW_TILE
    return pl.BlockSpec(
        (None, None, 3 * N_SUB, d),
        lambda b, t: (layer, 2 * b + jnp.minimum((t + tile_off) // n_ctx_tiles, 1), 0, 0))


def _modulated_norm(x, g, shift, scale):
    ms = jnp.mean(x * x, axis=-1, keepdims=True)
    return x * lax.rsqrt(ms + NORM_EPS) * g * (1.0 + scale) + shift


def _mod_kernel(c_ref, w_ref, b_ref, o_ref):
    c = c_ref[...]
    s = (c * jax.nn.sigmoid(c)).astype(BF16)
    o_ref[...] = jnp.dot(s, w_ref[...].astype(BF16), preferred_element_type=F32) + b_ref[...]


def _modulation(c8, w_mod, b_mod):
    depth, d, n = w_mod.shape
    tn = 1024
    return pl.pallas_call(
        _mod_kernel,
        grid=(depth, n // tn),
        in_specs=[
            pl.BlockSpec((SUBLANES, d), lambda l, j: (0, 0)),
            pl.BlockSpec((None, d, tn), lambda l, j: (l, 0, j)),
            pl.BlockSpec((None, 1, tn), lambda l, j: (l, 0, j)),
        ],
        out_specs=pl.BlockSpec((None, SUBLANES, tn), lambda l, j: (l, 0, j)),
        out_shape=jax.ShapeDtypeStruct((depth, SUBLANES, n), F32),
        compiler_params=_cparams("parallel", "parallel"),
        name="adaln_modulation",
    )(c8, w_mod, b_mod.reshape(depth, 1, n))


def _ffn_kernel(x_ref, mod_ref, g_ref, w1_ref, w2_ref, *rest, sub, final_norm):
    if final_norm:
        fg_ref, o_ref = rest
    else:
        (o_ref,) = rest
    x = x_ref[...]
    shift = mod_ref[3 * sub + 0:3 * sub + 1, :]
    scale = mod_ref[3 * sub + 1:3 * sub + 2, :]
    gate = mod_ref[3 * sub + 2:3 * sub + 3, :]
    hb = _modulated_norm(x, g_ref[...], shift, scale).astype(BF16)
    acc = jnp.zeros(x.shape, F32)
    n_chunks = D_FF // FFN_CHUNK

    def up(c):
        return jnp.dot(hb, w1_ref[:, 2 * c * FFN_CHUNK:2 * (c + 1) * FFN_CHUNK], preferred_element_type=F32)

    ab = up(0)
    for c in range(n_chunks):
        ab_next = up(c + 1) if c + 1 < n_chunks else None
        a, b = ab[:, :FFN_CHUNK], ab[:, FFN_CHUNK:]
        act = (a * jax.nn.sigmoid(a) * b).astype(BF16)
        acc = acc + jnp.dot(act, w2_ref[c * FFN_CHUNK:(c + 1) * FFN_CHUNK, :], preferred_element_type=F32)
        ab = ab_next
    y = x + (MACARON_WEIGHT * gate) * acc
    if final_norm:
        ms = jnp.mean(y * y, axis=-1, keepdims=True)
        y = y * lax.rsqrt(ms + NORM_EPS) * fg_ref[...]
    o_ref[...] = y


def _ffn(xs, mods, norm_g, w1, w2, *, layer, which, tile_off=0, final_g=None):
    bsz, t_all, d = xs.shape
    sub = 2 * which
    nt = t_all // ROW_TILE - tile_off
    in_specs = [
        pl.BlockSpec((None, ROW_TILE, d), lambda b, t: (b, t + tile_off, 0)),
        _mod_spec(layer, d, tile_off),
        _pinned(norm_g.shape, layer, sub),
        _pinned(w1.shape, layer, which),
        _pinned(w2.shape, layer, which),
    ]
    args = [xs, mods, norm_g, w1, w2]
    if final_g is not None:
        in_specs.append(_pinned(final_g.shape))
        args.append(final_g)
    return pl.pallas_call(
        functools.partial(_ffn_kernel, sub=sub, final_norm=final_g is not None),
        grid=(bsz, nt),
        in_specs=in_specs,
        out_specs=pl.BlockSpec((None, ROW_TILE, d), lambda b, t: (b, t, 0)),
        out_shape=jax.ShapeDtypeStruct((bsz, nt * ROW_TILE, d), F32),
        compiler_params=_cparams("parallel", "parallel"),
        name="swiglu_half_step",
    )(*args)


def _rope_block(x, cos, sin_signed, half):
    lane = lax.broadcasted_iota(jnp.int32, x.shape, 1)
    up = pltpu.roll(x, LANES - half, axis=1)
    down = pltpu.roll(x, half, axis=1)
    partner = jnp.where((lane & half) == 0, up, down)
    return x * cos + partner * sin_signed


def _head_pair_rms(x, g):
    lane = lax.broadcasted_iota(jnp.int32, x.shape, 1)
    first = lane < ATT_HEAD_DIM
    sq = x * x
    s0 = jnp.sum(jnp.where(first, sq, 0.0), axis=-1, keepdims=True)
    s1 = jnp.sum(jnp.where(first, 0.0, sq), axis=-1, keepdims=True)
    ms = jnp.where(first, s0, s1) * (1.0 / ATT_HEAD_DIM)
    return x * lax.rsqrt(ms + NORM_EPS) * g


def _proj_kernel(x_ref, mod_ref, g_ref, w_ref, rcos_ref, rsin_ref, acos_ref, asin_ref,
                 qg_ref, kg_ref,
                 rq_ref, rk_ref, rv_ref, rg_ref, lx_ref, lz_ref, aq_ref, ak_ref, avt_ref):
    x = x_ref[...]
    hb = _modulated_norm(x, g_ref[...], mod_ref[3:4, :], mod_ref[4:5, :]).astype(BF16)
    w = BRANCH_WIDTH

    def cols(lo, n):
        return jnp.dot(hb, w_ref[:, lo:lo + n], preferred_element_type=F32)

    rcos, rsin = rcos_ref[...], rsin_ref[...]
    acos, asin = acos_ref[...], asin_ref[...]
    k_scale = RET_HEAD_DIM ** -0.5
    zq = cols(0, w)
    zk = cols(w, w)
    za = cols(6 * w, w)
    zkv = cols(7 * w, 2 * LANES)
    for h in range(RET_HEADS):
        sl = slice(h * LANES, (h + 1) * LANES)
        rq_ref[:, sl] = _rope_block(zq[:, sl], rcos, rsin, RET_HEAD_DIM // 4).astype(BF16)
        rk_ref[:, sl] = (_rope_block(zk[:, sl], rcos, rsin, RET_HEAD_DIM // 4) * k_scale).astype(BF16)
    rv_ref[...] = cols(2 * w, w).astype(BF16)
    rg_ref[...] = cols(3 * w, w)
    lx_ref[...] = cols(4 * w, w)
    lz_ref[...] = cols(5 * w, w)
    q_scale = ATT_HEAD_DIM ** -0.5 * LOG2_E
    for p in range(ATT_Q_HEADS // 2):
        sl = slice(p * LANES, (p + 1) * LANES)
        qn = _head_pair_rms(za[:, sl], qg_ref[...])
        aq_ref[:, sl] = (_rope_block(qn, acos, asin, ATT_HEAD_DIM // 4) * q_scale).astype(BF16)
    kn = _rope_block(_head_pair_rms(zkv[:, :LANES], kg_ref[...]), acos, asin, ATT_HEAD_DIM // 4)
    for j in range(ATT_KV_HEADS):
        ak_ref[j] = kn[:, j * ATT_HEAD_DIM:(j + 1) * ATT_HEAD_DIM].astype(BF16)
    vt = zkv[:, LANES:].T
    for j in range(ATT_KV_HEADS):
        avt_ref[j] = vt[j * ATT_HEAD_DIM:(j + 1) * ATT_HEAD_DIM, :].astype(BF16)


def _proj(xs, mods, norm_g, w_mix, rope, qg, kg, *, layer):
    bsz, t_all, d = xs.shape
    nt = t_all // ROW_TILE
    w = BRANCH_WIDTH
    tile = lambda n: pl.BlockSpec((None, ROW_TILE, n), lambda b, t: (b, t, 0))
    table = pl.BlockSpec((ROW_TILE, LANES), lambda b, t: (t, 0))
    kv_spec = pl.BlockSpec((None, ATT_KV_HEADS, ROW_TILE, ATT_HEAD_DIM), lambda b, t: (b, 0, t, 0))
    sds = lambda n, dt: jax.ShapeDtypeStruct((bsz, t_all, n), dt)
    kv_sds = jax.ShapeDtypeStruct((bsz, ATT_KV_HEADS, t_all, ATT_HEAD_DIM), BF16)
    vt_spec = pl.BlockSpec((None, ATT_KV_HEADS, ATT_HEAD_DIM, ROW_TILE), lambda b, t: (b, 0, 0, t))
    vt_sds = jax.ShapeDtypeStruct((bsz, ATT_KV_HEADS, ATT_HEAD_DIM, t_all), BF16)
    return pl.pallas_call(
        _proj_kernel,
        grid=(bsz, nt),
        in_specs=[
            tile(d),
            _mod_spec(layer, d),
            _pinned(norm_g.shape, layer, 1),
            _pinned(w_mix.shape, layer),
            table, table, table, table,
            _pinned(qg.shape, layer), _pinned(kg.shape, layer),
        ],
        out_specs=[tile(w), tile(w), tile(w), tile(w), tile(w), tile(w), tile(w), kv_spec, vt_spec],
        out_shape=[sds(w, BF16), sds(w, BF16), sds(w, BF16), sds(w, F32), sds(w, F32), sds(w, F32),
                   sds(w, BF16), kv_sds, vt_sds],
        compiler_params=_cparams("parallel", "parallel"),
        name="mixer_input_projection",
    )(xs, mods, norm_g, w_mix, *rope, qg, kg)


def _log_sigmoid(x):
    return jnp.minimum(x, 0.0) - jnp.log1p(jnp.exp(-jnp.abs(x)))


def _retention_kernel(q_ref, k_ref, v_ref, rg_ref, dl_ref, ng_ref, o_ref, s_ref):
    hb = pl.program_id(1)
    c = RET_CHUNK
    n_chunks = q_ref.shape[0] // c
    n_ctx = CTX_LEN // c
    row = lax.broadcasted_iota(jnp.int32, (c, c), 0).astype(F32)
    col = lax.broadcasted_iota(jnp.int32, (c, c), 1).astype(F32)
    diff = row - col

    consts = []
    for j in range(RET_HEADS_PER_STEP):
        h = hb * RET_HEADS_PER_STEP + j
        lg_f = _log_sigmoid(dl_ref[pl.ds(h, 1), :])
        lg_b = _log_sigmoid(dl_ref[pl.ds(RET_HEADS + h, 1), :])
        consts.append(dict(
            intra=(jnp.where(diff >= 0, jnp.exp(jnp.maximum(diff, 0.0) * lg_f), 0.0)
                   + jnp.where(diff <= 0, jnp.exp(jnp.maximum(-diff, 0.0) * lg_b), 0.0)),
            qd_f=jnp.exp((row + 1.0) * lg_f), kd_f=jnp.exp((c - 1.0 - row) * lg_f),
            qd_b=jnp.exp((c - row) * lg_b), kd_b=jnp.exp(row * lg_b),
            sd_f=jnp.exp(c * lg_f), sd_b=jnp.exp(c * lg_b)))

    def state_step(j, ci, s, kd, sd, lane0):
        r0 = pl.multiple_of(ci * c, c)
        sl = slice(j * LANES, (j + 1) * LANES)
        s_ref[j, ci, :, lane0:lane0 + LANES] = s.astype(BF16)
        kd_c = (k_ref[pl.ds(r0, c), sl].astype(F32) * kd).astype(BF16)
        kv = lax.dot_general(kd_c, v_ref[pl.ds(r0, c), sl], (((0,), (0,)), ((), ())),
                             preferred_element_type=F32)
        return s * sd + kv

    def pass1(i, carry):
        cb = jnp.where(i < n_ctx, n_ctx - 1 - i, n_chunks + n_ctx - 1 - i)
        out = []
        for j in range(RET_HEADS_PER_STEP):
            k = consts[j]
            s_f, s_b = carry[j]
            out.append((state_step(j, i, s_f, k["kd_f"], k["sd_f"], 0),
                        state_step(j, cb, s_b, k["kd_b"], k["sd_b"], LANES)))
        return tuple(out)

    zero = jnp.zeros((c, c), F32)
    lax.fori_loop(0, n_chunks, pass1, tuple((zero, zero) for _ in range(RET_HEADS_PER_STEP)), unroll=2)

    def pass2(ci, _):
        r0 = pl.multiple_of(ci * c, c)
        for j in range(RET_HEADS_PER_STEP):
            k = consts[j]
            sl = slice(j * LANES, (j + 1) * LANES)
            qc = q_ref[pl.ds(r0, c), sl]
            kc = k_ref[pl.ds(r0, c), sl]
            vc = v_ref[pl.ds(r0, c), sl]
            scores = lax.dot_general(qc, kc, (((1,), (1,)), ((), ())), preferred_element_type=F32) * k["intra"]
            inner = jnp.dot(scores.astype(BF16), vc, preferred_element_type=F32)
            cross = jnp.dot(qc, s_ref[j, ci], preferred_element_type=F32)
            y = inner + cross[:, :LANES] * k["qd_f"] + cross[:, LANES:] * k["qd_b"]
            mu = jnp.mean(y, axis=-1, keepdims=True)
            yc = y - mu
            var = jnp.mean(yc * yc, axis=-1, keepdims=True)
            yn = yc * lax.rsqrt(var + NORM_EPS) * ng_ref[:, sl]
            gate = rg_ref[pl.ds(r0, c), sl]
            o_ref[pl.ds(r0, c), sl] = (yn * (gate * jax.nn.sigmoid(gate))).astype(BF16)
        return 0

    lax.fori_loop(0, n_chunks, pass2, 0, unroll=2)


def _retention(rq, rk, rv, rg, decay_lanes, norm_g, *, layer):
    bsz, t_all, _ = rq.shape
    w = RET_HEADS_PER_STEP * LANES
    head = pl.BlockSpec((None, t_all, w), lambda b, h: (b, 0, h))
    return pl.pallas_call(
        _retention_kernel,
        grid=(bsz, RET_HEADS // RET_HEADS_PER_STEP),
        in_specs=[head, head, head, head,
                  pl.BlockSpec((None, 2 * RET_HEADS, LANES), lambda b, h: (layer, 0, 0)),
                  pl.BlockSpec((None, 1, w), lambda b, h: (layer, 0, h))],
        out_specs=head,
        out_shape=jax.ShapeDtypeStruct((bsz, t_all, BRANCH_WIDTH), BF16),
        scratch_shapes=[pltpu.VMEM((RET_HEADS_PER_STEP, t_all // RET_CHUNK, RET_HEAD_DIM, 2 * RET_HEAD_DIM), BF16)],
        compiler_params=_cparams("parallel", "parallel"),
        name="bidirectional_retention",
    )(rq, rk, rv, rg, decay_lanes, norm_g)


def _lru_kernel(lx_ref, lz_ref, cw_ref, cb_ref, w_ref, b_ref, lam_ref, o_ref,
                xp_ref, xs_ref, hf_ref, hb_ref):
    t_all = lx_ref.shape[0]
    ch = LRU_CHUNK
    n_chunks = t_all // ch
    pad = SUBLANES

    xp_ref[0:pad, :] = jnp.zeros((pad, LANES), F32)
    xp_ref[pad + t_all:pad + t_all + pad, :] = jnp.zeros((pad, LANES), F32)
    xp_ref[pad:pad + t_all, :] = lx_ref[...]
    cw = cw_ref[...]
    cb = cb_ref[...]
    for k in range(n_chunks):
        r0 = k * ch
        r = r0 + lax.broadcasted_iota(jnp.int32, (ch, LANES), 0)
        xm1 = xp_ref[pad + r0 - 1:pad + r0 - 1 + ch, :]
        x0 = xp_ref[pad + r0:pad + r0 + ch, :]
        xp1 = xp_ref[pad + r0 + 1:pad + r0 + 1 + ch, :]
        xp2 = xp_ref[pad + r0 + 2:pad + r0 + 2 + ch, :]
        if r0 <= CTX_LEN < r0 + ch:
            xm1 = jnp.where(r == CTX_LEN, 0.0, xm1)
        if r0 < CTX_LEN <= r0 + ch:
            xp1 = jnp.where(r == CTX_LEN - 1, 0.0, xp1)
            xp2 = jnp.where((r == CTX_LEN - 1) | (r == CTX_LEN - 2), 0.0, xp2)
        xs_ref[r0:r0 + ch, :] = cb + xm1 * cw[0:1] + x0 * cw[1:2] + xp1 * cw[2:3] + xp2 * cw[3:4]

    lam = lam_ref[...]
    sp = jnp.maximum(-lam, 0.0) + jnp.log1p(jnp.exp(-jnp.abs(lam)))
    sub = lax.broadcasted_iota(jnp.int32, (ch, LANES), 0) % SUBLANES
    n_groups = ch // SUBLANES

    def scan_chunk(ci, carry, d, out_ref):
        r0 = pl.multiple_of(ci * ch, ch)
        x = xs_ref[pl.ds(r0, ch), :]
        pre = jnp.dot(x.astype(BF16), w_ref[:, 2 * d * LANES:(2 * d + 2) * LANES],
                      preferred_element_type=F32) + b_ref[:, 2 * d * LANES:(2 * d + 2) * LANES]
        r_gate = jax.nn.sigmoid(pre[:, :LANES])
        i_gate = jax.nn.sigmoid(pre[:, LANES:])
        log_a = -LRU_C * r_gate * sp[d:d + 1]
        a = jnp.exp(log_a)
        th = jnp.tanh(log_a)
        u = jnp.sqrt(2.0 * th / (th - 1.0)) * (i_gate * x)
        for s in (1, 2, 4):
            if d == 0:
                a_sh = pltpu.roll(a, s, axis=0)
                u_sh = pltpu.roll(u, s, axis=0)
                ok = sub >= s
            else:
                a_sh = pltpu.roll(a, ch - s, axis=0)
                u_sh = pltpu.roll(u, ch - s, axis=0)
                ok = sub < SUBLANES - s
            u = jnp.where(ok, a * u_sh + u, u)
            a = jnp.where(ok, a * a_sh, a)
        groups = range(n_groups) if d == 0 else range(n_groups - 1, -1, -1)
        edge = SUBLANES - 1 if d == 0 else 0
        for gi in groups:
            lo = gi * SUBLANES
            hblk = u[lo:lo + SUBLANES] + a[lo:lo + SUBLANES] * carry
            out_ref[pl.ds(pl.multiple_of(r0 + lo, SUBLANES), SUBLANES), :] = hblk
            carry = hblk[edge:edge + 1]
        return carry

    n_ctx = CTX_LEN // ch

    def body(i, carry):
        c_f, c_b = carry
        c_f = scan_chunk(i, c_f, 0, hf_ref)
        cb_idx = jnp.where(i < n_ctx, n_ctx - 1 - i, n_chunks + n_ctx - 1 - i)
        c_b = scan_chunk(cb_idx, c_b, 1, hb_ref)
        return c_f, c_b

    zero = jnp.zeros((1, LANES), F32)
    lax.fori_loop(0, n_chunks, body, (zero, zero))

    def finish(i, _):
        r0 = pl.multiple_of(i * ch, ch)
        z = lz_ref[pl.ds(r0, ch), :]
        hsum = hf_ref[pl.ds(r0, ch), :] + hb_ref[pl.ds(r0, ch), :]
        o_ref[pl.ds(r0, ch), :] = (jax.nn.gelu(z) * hsum).astype(BF16)
        return 0

    lax.fori_loop(0, n_chunks, finish, 0)


def _lru(lx, lz, conv_w, conv_b, w_gates, b_gates, lam, *, layer):
    bsz, t_all, width = lx.shape
    nb = width // LANES
    col = pl.BlockSpec((None, t_all, LANES), lambda b, j: (b, 0, j))
    return pl.pallas_call(
        _lru_kernel,
        grid=(bsz, nb),
        in_specs=[col, col,
                  pl.BlockSpec((None, 4, LANES), lambda b, j: (layer, 0, j)),
                  pl.BlockSpec((None, 1, LANES), lambda b, j: (layer, 0, j)),
                  pl.BlockSpec((None, None, LANES, 4 * LANES), lambda b, j: (layer, j, 0, 0)),
                  pl.BlockSpec((None, None, 1, 4 * LANES), lambda b, j: (layer, j, 0, 0)),
                  pl.BlockSpec((None, 2, LANES), lambda b, j: (layer, 0, j))],
        out_specs=col,
        out_shape=jax.ShapeDtypeStruct((bsz, t_all, width), BF16),
        scratch_shapes=[pltpu.VMEM((t_all + 2 * SUBLANES, LANES), F32)] + [pltpu.VMEM((t_all, LANES), F32)] * 3,
        compiler_params=_cparams("parallel", "parallel"),
        name="bidirectional_rglru",
    )(lx, lz, conv_w, conv_b, w_gates, b_gates, lam)


def _attention_kernel(q_ref, k_ref, vt_ref, o_ref):
    qt = pl.program_id(2)
    hd = ATT_HEAD_DIM
    rows = q_ref.shape[0]

    def attend(n_keys):
        k = k_ref[0:n_keys, :]
        vt = jnp.concatenate([vt_ref[:, 0:n_keys], jnp.ones((2 * SUBLANES, n_keys), BF16)], axis=0)
        outs = []
        scores = []
        for pair in range(ATT_GROUP // 2):
            q2 = jnp.concatenate([q_ref[:, (2 * pair) * hd:(2 * pair + 1) * hd],
                                  q_ref[:, (2 * pair + 1) * hd:(2 * pair + 2) * hd]], axis=0)
            scores.append(lax.dot_general(k, q2, (((1,), (1,)), ((), ())),
                                          preferred_element_type=F32))
        for st in scores:
            m = jnp.max(st, axis=0, keepdims=True)
            p = jnp.exp2(st - m).astype(BF16)
            ot = jnp.dot(vt, p, preferred_element_type=F32)
            ot = ot[:hd] / ot[hd:hd + 1]
            outs += [ot[:, :rows].T, ot[:, rows:].T]
        o_ref[...] = jnp.concatenate(outs, axis=1).astype(BF16)

    @pl.when(qt < CTX_LEN // ROW_TILE)
    def _():
        attend(CTX_LEN)

    @pl.when(qt >= CTX_LEN // ROW_TILE)
    def _():
        attend(k_ref.shape[0])


def _attention(aq, ak, avt):
    bsz, t_all, _ = aq.shape
    gw = ATT_GROUP * ATT_HEAD_DIM
    qo = pl.BlockSpec((None, ROW_TILE, gw), lambda b, g, t: (b, t, g))
    k_spec = pl.BlockSpec((None, None, t_all, ATT_HEAD_DIM), lambda b, g, t: (b, g, 0, 0))
    vt_spec = pl.BlockSpec((None, None, ATT_HEAD_DIM, t_all), lambda b, g, t: (b, g, 0, 0))
    return pl.pallas_call(
        _attention_kernel,
        grid=(bsz, ATT_KV_HEADS, t_all // ROW_TILE),
        in_specs=[qo, k_spec, vt_spec],
        out_specs=qo,
        out_shape=jax.ShapeDtypeStruct((bsz, t_all, BRANCH_WIDTH), BF16),
        compiler_params=_cparams("parallel", "parallel", "parallel"),
        name="prefix_gqa_attention",
    )(aq, ak, avt)


def _merge_kernel(x_ref, mod_ref, g_ref, yr_ref, yl_ref, ya_ref, wg_ref, wb_ref, wo_ref, o_ref):
    x = x_ref[...]
    hb = _modulated_norm(x, g_ref[...], mod_ref[3:4, :], mod_ref[4:5, :]).astype(BF16)
    d = x.shape[-1]
    mix = jnp.zeros(x.shape, F32)
    for n, y_ref in enumerate((yr_ref, yl_ref, ya_ref)):
        gate = jax.nn.sigmoid(jnp.dot(hb, wg_ref[:, n * d:(n + 1) * d], preferred_element_type=F32))
        mix = mix + gate * jnp.dot(y_ref[...], wb_ref[n], preferred_element_type=F32)
    y = jnp.dot(mix.astype(BF16), wo_ref[...], preferred_element_type=F32)
    o_ref[...] = x + mod_ref[5:6, :] * y


def _merge(xs, mods, norm_g, y_ret, y_lru, y_att, w_gate, w_branch, w_out, *, layer):
    bsz, t_all, d = xs.shape
    tile = lambda n: pl.BlockSpec((None, ROW_TILE, n), lambda b, t: (b, t, 0))
    return pl.pallas_call(
        _merge_kernel,
        grid=(bsz, t_all // ROW_TILE),
        in_specs=[tile(d),
                  _mod_spec(layer, d),
                  _pinned(norm_g.shape, layer, 1),
                  tile(BRANCH_WIDTH), tile(BRANCH_WIDTH), tile(BRANCH_WIDTH),
                  _pinned(w_gate.shape, layer), _pinned(w_branch.shape, layer), _pinned(w_out.shape, layer)],
        out_specs=tile(d),
        out_shape=jax.ShapeDtypeStruct(xs.shape, F32),
        compiler_params=_cparams("parallel", "parallel"),
        name="gated_branch_merge",
    )(xs, mods, norm_g, y_ret, y_lru, y_att, w_gate, w_branch, w_out)


def _rope_tables(seq, head_dim):
    half = head_dim // 2
    quarter = half // 2
    freqs = ROPE_THETA ** (-jnp.arange(0, half, 2, dtype=F32) / half)
    rows = jnp.repeat(jnp.arange(seq // GRID_W, dtype=F32), GRID_W)
    cols = jnp.tile(jnp.arange(GRID_W, dtype=F32), seq // GRID_W)
    ang_r = rows[:, None] * freqs[None]
    ang_c = cols[:, None] * freqs[None]
    ang = jnp.concatenate([ang_r, ang_r, ang_c, ang_c], axis=-1)
    sign = jnp.tile(jnp.concatenate([-jnp.ones((quarter,), F32), jnp.ones((quarter,), F32)]), 2)
    cos = jnp.tile(jnp.cos(ang), (1, LANES // head_dim))
    sin = jnp.tile(jnp.sin(ang) * sign[None], (1, LANES // head_dim))
    cos = jnp.concatenate([jnp.ones((CTX_LEN, LANES), F32), cos], axis=0)
    sin = jnp.concatenate([jnp.zeros((CTX_LEN, LANES), F32), sin], axis=0)
    return cos, sin


def _block_diag_lanes(w):
    depth = w.shape[0]
    per = LANES // LRU_BLOCK
    w = w.reshape(depth, 2, LRU_BLOCKS // per, per, LRU_BLOCK, LRU_BLOCK)
    eye = jnp.eye(per, dtype=w.dtype)
    dense = jnp.einsum('ldjpab,pq->ldjpaqb', w, eye).reshape(depth, 2, LRU_BLOCKS // per, LANES, LANES)
    return dense.transpose(0, 2, 1, 3, 4)


def kernel(x, c, ctx, c_ctx, w_mod, b_mod, norm_g, ffn_w_in, ffn_w_out, w_in,
           ret_decay_logit, ret_norm_g, lru_conv_w, lru_conv_b, lru_w_a, lru_b_a,
           lru_w_x, lru_b_x, lru_lambda, attn_q_norm_g, attn_k_norm_g, w_branch,
           w_out, final_norm_g):
    bsz, seq, d = x.shape
    depth = w_mod.shape[0]
    nb = LRU_WIDTH // LANES
    xs = jnp.concatenate([ctx, x], axis=1)

    c8 = jnp.concatenate([c, c_ctx[None], jnp.zeros((SUBLANES - bsz - 1, d), F32)], axis=0)
    m = _modulation(c8, w_mod, b_mod).reshape(depth, SUBLANES, 3 * N_SUB, d)
    m_ctx = jnp.broadcast_to(m[:, bsz:bsz + 1], (depth, bsz, 3 * N_SUB, d))
    mods = jnp.stack([m_ctx, m[:, :bsz]], axis=2).reshape(depth, 2 * bsz, 3 * N_SUB, d)

    g_all = norm_g.reshape(depth, N_SUB, 1, d)
    n_ff = D_FF // FFN_CHUNK
    w1 = ffn_w_in.reshape(depth, 2, d, 2, n_ff, FFN_CHUNK).transpose(0, 1, 2, 4, 3, 5)
    w1 = w1.reshape(depth, 2, d, 2 * D_FF).astype(BF16)
    w2 = ffn_w_out.astype(BF16)
    w_mix = w_in[:, :, :MIX_COLS].astype(BF16)
    w_gate = w_in[:, :, MIX_COLS:].astype(BF16)
    wb = w_branch.astype(BF16)
    wo = w_out.astype(BF16)
    rope = _rope_tables(seq, RET_HEAD_DIM) + _rope_tables(seq, ATT_HEAD_DIM)
    qg = jnp.tile(attn_q_norm_g, (1, 2))[:, None, :]
    kg = jnp.tile(attn_k_norm_g, (1, 2))[:, None, :]
    decay_lanes = jnp.broadcast_to(ret_decay_logit.reshape(depth, 2 * RET_HEADS, 1), (depth, 2 * RET_HEADS, LANES))
    ret_g = ret_norm_g[:, None, :]
    wa = _block_diag_lanes(lru_w_a)
    wx = _block_diag_lanes(lru_w_x)
    w_gates = jnp.concatenate([wa[:, :, 0], wx[:, :, 0], wa[:, :, 1], wx[:, :, 1]], axis=-1).astype(BF16)
    ba = lru_b_a.reshape(depth, 2, nb, LANES)
    bx = lru_b_x.reshape(depth, 2, nb, LANES)
    b_gates = jnp.concatenate([ba[:, 0], bx[:, 0], ba[:, 1], bx[:, 1]], axis=-1)[:, :, None, :]
    conv_b = lru_conv_b[:, None, :]
    final_g = final_norm_g[None]

    for l in range(depth):
        last = l == depth - 1
        xs = _ffn(xs, mods, g_all, w1, w2, layer=l, which=0)
        rq, rk, rv, rg, lx, lz, aq, ak, av = _proj(xs, mods, g_all, w_mix, rope, qg, kg, layer=l)
        y_ret = _retention(rq, rk, rv, rg, decay_lanes, ret_g, layer=l)
        y_lru = _lru(lx, lz, lru_conv_w, conv_b, w_gates, b_gates, lru_lambda, layer=l)
        y_att = _attention(aq, ak, av)
        xs = _merge(xs, mods, g_all, y_ret, y_lru, y_att, w_gate, wb, wo, layer=l)
        if last:
            xs = _ffn(xs, mods, g_all, w1, w2, layer=l, which=1, tile_off=CTX_LEN // ROW_TILE,
                      final_g=final_g)
        else:
            xs = _ffn(xs, mods, g_all, w1, w2, layer=l, which=1)
    return xs
```

```python
import functools

import jax
import jax.numpy as jnp
from jax import lax
from jax.experimental import pallas as pl
from jax.experimental.pallas import tpu as pltpu

D_MODEL = 1024
GRID_W = 64
CTX_LEN = 256
N_BRANCH = 3
BRANCH_WIDTH = 512
RET_HEADS = 4
RET_HEAD_DIM = 128
RET_CHUNK = 128
LRU_WIDTH = 512
LRU_BLOCKS = 8
LRU_BLOCK = 64
LRU_C = 8.0
ATT_HEAD_DIM = 64
ATT_Q_HEADS = 8
ATT_KV_HEADS = 2
ATT_GROUP = 4
ROPE_THETA = 10000.0
D_FF = 2816
MACARON_WEIGHT = 0.5
N_SUB = 3
NORM_EPS = 1e-6
LOG2_E = 1.4426950408889634
MIX_COLS = 4 * BRANCH_WIDTH + 2 * LRU_WIDTH + BRANCH_WIDTH + 2 * ATT_HEAD_DIM * ATT_KV_HEADS

LANES = 128
SUBLANES = 8
VMEM_LIMIT = 56 * 1024 * 1024

ROW_TILE = 256
FFN_CHUNK = 256
LRU_CHUNK = 256
RET_HEADS_PER_STEP = 2

BF16 = jnp.bfloat16
F32 = jnp.float32


def _cparams(*sem):
    return pltpu.CompilerParams(dimension_semantics=sem, vmem_limit_bytes=VMEM_LIMIT)


def _pinned(shape, *lead):
    tail = tuple(shape[len(lead):])
    index = tuple(lead) + (0,) * len(tail)
    return pl.BlockSpec((None,) * len(lead) + tail, lambda *_: index, pipeline_mode=pl.Buffered(1))


def _mod_spec(layer, d, tile_off=0):
    n_ctx_tiles = CTX_LEN // ROW_TILE
    return pl.BlockSpec(
        (None, None, 3 * N_SUB, d),
        lambda b, t: (layer, 2 * b + jnp.minimum((t + tile_off) // n_ctx_tiles, 1), 0, 0))


def _modulated_norm(x, g, shift, scale):
    ms = jnp.mean(x * x, axis=-1, keepdims=True)
    return x * lax.rsqrt(ms + NORM_EPS) * g * (1.0 + scale) + shift


def _mod_kernel(c_ref, w_ref, b_ref, o_ref):
    c = c_ref[...]
    s = (c * jax.nn.sigmoid(c)).astype(BF16)
    o_ref[...] = jnp.dot(s, w_ref[...].astype(BF16), preferred_element_type=F32) + b_ref[...]


def _modulation(c8, w_mod, b_mod):
    depth, d, n = w_mod.shape
    tn = 1024
    return pl.pallas_call(
        _mod_kernel,
        grid=(depth, n // tn),
        in_specs=[
            pl.BlockSpec((SUBLANES, d), lambda l, j: (0, 0)),
            pl.BlockSpec((None, d, tn), lambda l, j: (l, 0, j)),
            pl.BlockSpec((None, 1, tn), lambda l, j: (l, 0, j)),
        ],
        out_specs=pl.BlockSpec((None, SUBLANES, tn), lambda l, j: (l, 0, j)),
        out_shape=jax.ShapeDtypeStruct((depth, SUBLANES, n), F32),
        compiler_params=_cparams("parallel", "parallel"),
        name="adaln_modulation",
    )(c8, w_mod, b_mod.reshape(depth, 1, n))


def _ffn_kernel(x_ref, mod_ref, g_ref, w1_ref, w2_ref, *rest, sub, final_norm):
    if final_norm:
        fg_ref, o_ref = rest
    else:
        (o_ref,) = rest
    x = x_ref[...]
    shift = mod_ref[3 * sub + 0:3 * sub + 1, :]
    scale = mod_ref[3 * sub + 1:3 * sub + 2, :]
    gate = mod_ref[3 * sub + 2:3 * sub + 3, :]
    hb = _modulated_norm(x, g_ref[...], shift, scale).astype(BF16)
    acc = jnp.zeros(x.shape, F32)
    n_chunks = D_FF // FFN_CHUNK

    def up(c):
        lo = c * FFN_CHUNK
        return (jnp.dot(hb, w1_ref[:, lo:lo + FFN_CHUNK], preferred_element_type=F32),
                jnp.dot(hb, w1_ref[:, D_FF + lo:D_FF + lo + FFN_CHUNK], preferred_element_type=F32))

    ab = up(0)
    for c in range(n_chunks):
        ab_next = up(c + 1) if c + 1 < n_chunks else None
        a, b = ab
        act = (a * jax.nn.sigmoid(a) * b).astype(BF16)
        acc = acc + jnp.dot(act, w2_ref[c * FFN_CHUNK:(c + 1) * FFN_CHUNK, :], preferred_element_type=F32)
        ab = ab_next
    y = x + (MACARON_WEIGHT * gate) * acc
    if final_norm:
        ms = jnp.mean(y * y, axis=-1, keepdims=True)
        y = y * lax.rsqrt(ms + NORM_EPS) * fg_ref[...]
    o_ref[...] = y


def _ffn(xs, mods, norm_g, w1, w2, *, layer, which, tile_off=0, final_g=None):
    bsz, t_all, d = xs.shape
    sub = 2 * which
    nt = t_all // ROW_TILE - tile_off
    in_specs = [
        pl.BlockSpec((None, ROW_TILE, d), lambda b, t: (b, t + tile_off, 0)),
        _mod_spec(layer, d, tile_off),
        _pinned(norm_g.shape, layer, sub),
        _pinned(w1.shape, layer, which),
        _pinned(w2.shape, layer, which),
    ]
    args = [xs, mods, norm_g, w1, w2]
    if final_g is not None:
        in_specs.append(_pinned(final_g.shape))
        args.append(final_g)
    return pl.pallas_call(
        functools.partial(_ffn_kernel, sub=sub, final_norm=final_g is not None),
        grid=(bsz, nt),
        in_specs=in_specs,
        out_specs=pl.BlockSpec((None, ROW_TILE, d), lambda b, t: (b, t, 0)),
        out_shape=jax.ShapeDtypeStruct((bsz, nt * ROW_TILE, d), F32),
        compiler_params=_cparams("parallel", "parallel"),
        name="swiglu_half_step",
    )(*args)


def _rope_block(x, cos, sin_signed, half):
    lane = lax.broadcasted_iota(jnp.int32, x.shape, 1)
    up = pltpu.roll(x, LANES - half, axis=1)
    down = pltpu.roll(x, half, axis=1)
    partner = jnp.where((lane & half) == 0, up, down)
    return x * cos + partner * sin_signed


def _head_pair_rms(x, g):
    lane = lax.broadcasted_iota(jnp.int32, x.shape, 1)
    first = lane < ATT_HEAD_DIM
    sq = x * x
    s0 = jnp.sum(jnp.where(first, sq, 0.0), axis=-1, keepdims=True)
    s1 = jnp.sum(jnp.where(first, 0.0, sq), axis=-1, keepdims=True)
    ms = jnp.where(first, s0, s1) * (1.0 / ATT_HEAD_DIM)
    return x * lax.rsqrt(ms + NORM_EPS) * g


def _proj_kernel(x_ref, mod_ref, g_ref, w_ref, rcos_ref, rsin_ref, acos_ref, asin_ref,
                 qg_ref, kg_ref,
                 rq_ref, rk_ref, rv_ref, rg_ref, lx_ref, lz_ref, aq_ref, ak_ref, avt_ref):
    x = x_ref[...]
    hb = _modulated_norm(x, g_ref[...], mod_ref[3:4, :], mod_ref[4:5, :]).astype(BF16)
    w = BRANCH_WIDTH

    def cols(lo, n):
        return jnp.dot(hb, w_ref[:, lo:lo + n], preferred_element_type=F32)

    rcos, rsin = rcos_ref[...], rsin_ref[...]
    acos, asin = acos_ref[...], asin_ref[...]
    k_scale = RET_HEAD_DIM ** -0.5
    zq = cols(0, w)
    zk = cols(w, w)
    za = cols(6 * w, w)
    zkv = cols(7 * w, 2 * LANES)
    for h in range(RET_HEADS):
        sl = slice(h * LANES, (h + 1) * LANES)
        rq_ref[:, sl] = _rope_block(zq[:, sl], rcos, rsin, RET_HEAD_DIM // 4).astype(BF16)
        rk_ref[:, sl] = (_rope_block(zk[:, sl], rcos, rsin, RET_HEAD_DIM // 4) * k_scale).astype(BF16)
    rv_ref[...] = cols(2 * w, w).astype(BF16)
    rg_ref[...] = cols(3 * w, w)
    lx_ref[...] = cols(4 * w, w)
    lz_ref[...] = cols(5 * w, w)
    q_scale = ATT_HEAD_DIM ** -0.5 * LOG2_E
    for p in range(ATT_Q_HEADS // 2):
        sl = slice(p * LANES, (p + 1) * LANES)
        qn = _head_pair_rms(za[:, sl], qg_ref[...])
        aq_ref[:, sl] = (_rope_block(qn, acos, asin, ATT_HEAD_DIM // 4) * q_scale).astype(BF16)
    kn = _rope_block(_head_pair_rms(zkv[:, :LANES], kg_ref[...]), acos, asin, ATT_HEAD_DIM // 4)
    for j in range(ATT_KV_HEADS):
        ak_ref[j] = kn[:, j * ATT_HEAD_DIM:(j + 1) * ATT_HEAD_DIM].astype(BF16)
    vt = zkv[:, LANES:].T
    for j in range(ATT_KV_HEADS):
        avt_ref[j] = vt[j * ATT_HEAD_DIM:(j + 1) * ATT_HEAD_DIM, :].astype(BF16)


def _proj(xs, mods, norm_g, w_mix, rope, qg, kg, *, layer):
    bsz, t_all, d = xs.shape
    nt = t_all // ROW_TILE
    w = BRANCH_WIDTH
    tile = lambda n: pl.BlockSpec((None, ROW_TILE, n), lambda b, t: (b, t, 0))
    table = pl.BlockSpec((ROW_TILE, LANES), lambda b, t: (t, 0))
    kv_spec = pl.BlockSpec((None, ATT_KV_HEADS, ROW_TILE, ATT_HEAD_DIM), lambda b, t: (b, 0, t, 0))
    sds = lambda n, dt: jax.ShapeDtypeStruct((bsz, t_all, n), dt)
    kv_sds = jax.ShapeDtypeStruct((bsz, ATT_KV_HEADS, t_all, ATT_HEAD_DIM), BF16)
    vt_spec = pl.BlockSpec((None, ATT_KV_HEADS, ATT_HEAD_DIM, ROW_TILE), lambda b, t: (b, 0, 0, t))
    vt_sds = jax.ShapeDtypeStruct((bsz, ATT_KV_HEADS, ATT_HEAD_DIM, t_all), BF16)
    return pl.pallas_call(
        _proj_kernel,
        grid=(bsz, nt),
        in_specs=[
            tile(d),
            _mod_spec(layer, d),
            _pinned(norm_g.shape, layer, 1),
            _pinned(w_mix.shape, layer),
            table, table, table, table,
            _pinned(qg.shape, layer), _pinned(kg.shape, layer),
        ],
        out_specs=[tile(w), tile(w), tile(w), tile(w), tile(w), tile(w), tile(w), kv_spec, vt_spec],
        out_shape=[sds(w, BF16), sds(w, BF16), sds(w, BF16), sds(w, F32), sds(w, F32), sds(w, F32),
                   sds(w, BF16), kv_sds, vt_sds],
        compiler_params=_cparams("parallel", "parallel"),
        name="mixer_input_projection",
    )(xs, mods, norm_g, w_mix, *rope, qg, kg)


def _log_sigmoid(x):
    return jnp.minimum(x, 0.0) - jnp.log1p(jnp.exp(-jnp.abs(x)))


def _retention_kernel(q_ref, k_ref, v_ref, rg_ref, dl_ref, ng_ref, o_ref, s_ref):
    hb = pl.program_id(1)
    c = RET_CHUNK
    n_chunks = q_ref.shape[0] // c
    n_ctx = CTX_LEN // c
    row = lax.broadcasted_iota(jnp.int32, (c, c), 0).astype(F32)
    col = lax.broadcasted_iota(jnp.int32, (c, c), 1).astype(F32)
    diff = row - col

    consts = []
    for j in range(RET_HEADS_PER_STEP):
        h = hb * RET_HEADS_PER_STEP + j
        lg_f = _log_sigmoid(dl_ref[pl.ds(h, 1), :])
        lg_b = _log_sigmoid(dl_ref[pl.ds(RET_HEADS + h, 1), :])
        consts.append(dict(
            intra=(jnp.where(diff >= 0, jnp.exp(jnp.maximum(diff, 0.0) * lg_f), 0.0)
                   + jnp.where(diff <= 0, jnp.exp(jnp.maximum(-diff, 0.0) * lg_b), 0.0)),
            qd_f=jnp.exp((row + 1.0) * lg_f), kd_f=jnp.exp((c - 1.0 - row) * lg_f),
            qd_b=jnp.exp((c - row) * lg_b), kd_b=jnp.exp(row * lg_b),
            sd_f=jnp.exp(c * lg_f), sd_b=jnp.exp(c * lg_b)))

    def state_step(j, ci, s, kd, sd, lane0):
        r0 = pl.multiple_of(ci * c, c)
        sl = slice(j * LANES, (j + 1) * LANES)
        s_ref[j, ci, :, lane0:lane0 + LANES] = s.astype(BF16)
        kd_c = (k_ref[pl.ds(r0, c), sl].astype(F32) * kd).astype(BF16)
        kv = lax.dot_general(kd_c, v_ref[pl.ds(r0, c), sl], (((0,), (0,)), ((), ())),
                             preferred_element_type=F32)
        return s * sd + kv

    def pass1(i, carry):
        cb = jnp.where(i < n_ctx, n_ctx - 1 - i, n_chunks + n_ctx - 1 - i)
        out = []
        for j in range(RET_HEADS_PER_STEP):
            k = consts[j]
            s_f, s_b = carry[j]
            out.append((state_step(j, i, s_f, k["kd_f"], k["sd_f"], 0),
                        state_step(j, cb, s_b, k["kd_b"], k["sd_b"], LANES)))
        return tuple(out)

    zero = jnp.zeros((c, c), F32)
    lax.fori_loop(0, n_chunks, pass1, tuple((zero, zero) for _ in range(RET_HEADS_PER_STEP)), unroll=2)

    def pass2(ci, _):
        r0 = pl.multiple_of(ci * c, c)
        for j in range(RET_HEADS_PER_STEP):
            k = consts[j]
            sl = slice(j * LANES, (j + 1) * LANES)
            qc = q_ref[pl.ds(r0, c), sl]
            kc = k_ref[pl.ds(r0, c), sl]
            vc = v_ref[pl.ds(r0, c), sl]
            scores = lax.dot_general(qc, kc, (((1,), (1,)), ((), ())), preferred_element_type=F32) * k["intra"]
            inner = jnp.dot(scores.astype(BF16), vc, preferred_element_type=F32)
            cross = jnp.dot(qc, s_ref[j, ci], preferred_element_type=F32)
            y = inner + cross[:, :LANES] * k["qd_f"] + cross[:, LANES:] * k["qd_b"]
            mu = jnp.mean(y, axis=-1, keepdims=True)
            yc = y - mu
            var = jnp.mean(yc * yc, axis=-1, keepdims=True)
            yn = yc * lax.rsqrt(var + NORM_EPS) * ng_ref[:, sl]
            gate = rg_ref[pl.ds(r0, c), sl]
            o_ref[pl.ds(r0, c), sl] = (yn * (gate * jax.nn.sigmoid(gate))).astype(BF16)
        return 0

    lax.fori_loop(0, n_chunks, pass2, 0, unroll=2)


def _retention(rq, rk, rv, rg, decay_lanes, norm_g, *, layer):
    bsz, t_all, _ = rq.shape
    w = RET_HEADS_PER_STEP * LANES
    head = pl.BlockSpec((None, t_all, w), lambda b, h: (b, 0, h))
    return pl.pallas_call(
        _retention_kernel,
        grid=(bsz, RET_HEADS // RET_HEADS_PER_STEP),
        in_specs=[head, head, head, head,
                  pl.BlockSpec((None, 2 * RET_HEADS, LANES), lambda b, h: (layer, 0, 0)),
                  pl.BlockSpec((None, 1, w), lambda b, h: (layer, 0, h))],
        out_specs=head,
        out_shape=jax.ShapeDtypeStruct((bsz, t_all, BRANCH_WIDTH), BF16),
        scratch_shapes=[pltpu.VMEM((RET_HEADS_PER_STEP, t_all // RET_CHUNK, RET_HEAD_DIM, 2 * RET_HEAD_DIM), BF16)],
        compiler_params=_cparams("parallel", "parallel"),
        name="bidirectional_retention",
    )(rq, rk, rv, rg, decay_lanes, norm_g)


def _lru_kernel(lx_ref, lz_ref, cw_ref, cb_ref, w_ref, b_ref, lam_ref, o_ref,
                xp_ref, xs_ref, hf_ref, hb_ref):
    t_all = lx_ref.shape[0]
    ch = LRU_CHUNK
    n_chunks = t_all // ch
    pad = SUBLANES

    xp_ref[0:pad, :] = jnp.zeros((pad, LANES), F32)
    xp_ref[pad + t_all:pad + t_all + pad, :] = jnp.zeros((pad, LANES), F32)
    xp_ref[pad:pad + t_all, :] = lx_ref[...]
    cw = cw_ref[...]
    cb = cb_ref[...]
    for k in range(n_chunks):
        r0 = k * ch
        r = r0 + lax.broadcasted_iota(jnp.int32, (ch, LANES), 0)
        xm1 = xp_ref[pad + r0 - 1:pad + r0 - 1 + ch, :]
        x0 = xp_ref[pad + r0:pad + r0 + ch, :]
        xp1 = xp_ref[pad + r0 + 1:pad + r0 + 1 + ch, :]
        xp2 = xp_ref[pad + r0 + 2:pad + r0 + 2 + ch, :]
        if r0 <= CTX_LEN < r0 + ch:
            xm1 = jnp.where(r == CTX_LEN, 0.0, xm1)
        if r0 < CTX_LEN <= r0 + ch:
            xp1 = jnp.where(r == CTX_LEN - 1, 0.0, xp1)
            xp2 = jnp.where((r == CTX_LEN - 1) | (r == CTX_LEN - 2), 0.0, xp2)
        xs_ref[r0:r0 + ch, :] = cb + xm1 * cw[0:1] + x0 * cw[1:2] + xp1 * cw[2:3] + xp2 * cw[3:4]

    lam = lam_ref[...]
    sp = jnp.maximum(-lam, 0.0) + jnp.log1p(jnp.exp(-jnp.abs(lam)))
    sub = lax.broadcasted_iota(jnp.int32, (ch, LANES), 0) % SUBLANES
    n_groups = ch // SUBLANES

    def scan_chunk(ci, carry, d, out_ref):
        r0 = pl.multiple_of(ci * ch, ch)
        x = xs_ref[pl.ds(r0, ch), :]
        pre = jnp.dot(x.astype(BF16), w_ref[:, 2 * d * LANES:(2 * d + 2) * LANES],
                      preferred_element_type=F32) + b_ref[:, 2 * d * LANES:(2 * d + 2) * LANES]
        r_gate = jax.nn.sigmoid(pre[:, :LANES])
        i_gate = jax.nn.sigmoid(pre[:, LANES:])
        log_a = -LRU_C * r_gate * sp[d:d + 1]
        a = jnp.exp(log_a)
        th = jnp.tanh(log_a)
        u = jnp.sqrt(2.0 * th / (th - 1.0)) * (i_gate * x)
        for s in (1, 2, 4):
            if d == 0:
                a_sh = pltpu.roll(a, s, axis=0)
                u_sh = pltpu.roll(u, s, axis=0)
                ok = sub >= s
            else:
                a_sh = pltpu.roll(a, ch - s, axis=0)
                u_sh = pltpu.roll(u, ch - s, axis=0)
                ok = sub < SUBLANES - s
            u = jnp.where(ok, a * u_sh + u, u)
            a = jnp.where(ok, a * a_sh, a)
        groups = range(n_groups) if d == 0 else range(n_groups - 1, -1, -1)
        edge = SUBLANES - 1 if d == 0 else 0
        for gi in groups:
            lo = gi * SUBLANES
            hblk = u[lo:lo + SUBLANES] + a[lo:lo + SUBLANES] * carry
            out_ref[pl.ds(pl.multiple_of(r0 + lo, SUBLANES), SUBLANES), :] = hblk
            carry = hblk[edge:edge + 1]
        return carry

    n_ctx = CTX_LEN // ch

    def body(i, carry):
        c_f, c_b = carry
        c_f = scan_chunk(i, c_f, 0, hf_ref)
        cb_idx = jnp.where(i < n_ctx, n_ctx - 1 - i, n_chunks + n_ctx - 1 - i)
        c_b = scan_chunk(cb_idx, c_b, 1, hb_ref)
        return c_f, c_b

    zero = jnp.zeros((1, LANES), F32)
    lax.fori_loop(0, n_chunks, body, (zero, zero))

    def finish(i, _):
        r0 = pl.multiple_of(i * ch, ch)
        z = lz_ref[pl.ds(r0, ch), :]
        hsum = hf_ref[pl.ds(r0, ch), :] + hb_ref[pl.ds(r0, ch), :]
        o_ref[pl.ds(r0, ch), :] = (jax.nn.gelu(z) * hsum).astype(BF16)
        return 0

    lax.fori_loop(0, n_chunks, finish, 0)


def _lru(lx, lz, conv_w, conv_b, w_gates, b_gates, lam, *, layer):
    bsz, t_all, width = lx.shape
    nb = width // LANES
    col = pl.BlockSpec((None, t_all, LANES), lambda b, j: (b, 0, j))
    return pl.pallas_call(
        _lru_kernel,
        grid=(bsz, nb),
        in_specs=[col, col,
                  pl.BlockSpec((None, 4, LANES), lambda b, j: (layer, 0, j)),
                  pl.BlockSpec((None, 1, LANES), lambda b, j: (layer, 0, j)),
                  pl.BlockSpec((None, None, LANES, 4 * LANES), lambda b, j: (layer, j, 0, 0)),
                  pl.BlockSpec((None, None, 1, 4 * LANES), lambda b, j: (layer, j, 0, 0)),
                  pl.BlockSpec((None, 2, LANES), lambda b, j: (layer, 0, j))],
        out_specs=col,
        out_shape=jax.ShapeDtypeStruct((bsz, t_all, width), BF16),
        scratch_shapes=[pltpu.VMEM((t_all + 2 * SUBLANES, LANES), F32)] + [pltpu.VMEM((t_all, LANES), F32)] * 3,
        compiler_params=_cparams("parallel", "parallel"),
        name="bidirectional_rglru",
    )(lx, lz, conv_w, conv_b, w_gates, b_gates, lam)


def _attention_kernel(q_ref, k_ref, vt_ref, o_ref):
    qt = pl.program_id(1)
    hd = ATT_HEAD_DIM
    rows = q_ref.shape[0]

    def attend(n_keys):
        n_pairs = ATT_Q_HEADS // 2

        def score(i):
            g = (2 * i) // ATT_GROUP
            q2 = jnp.concatenate([q_ref[:, (2 * i) * hd:(2 * i + 1) * hd],
                                  q_ref[:, (2 * i + 1) * hd:(2 * i + 2) * hd]], axis=0)
            return lax.dot_general(k_ref[g, 0:n_keys, :], q2, (((1,), (1,)), ((), ())),
                                   preferred_element_type=F32)

        outs = []
        st = score(0)
        for i in range(n_pairs):
            st_next = score(i + 1) if i + 1 < n_pairs else None
            g = (2 * i) // ATT_GROUP
            vt = jnp.concatenate([vt_ref[g, :, 0:n_keys], jnp.ones((2 * SUBLANES, n_keys), BF16)], axis=0)
            m = jnp.max(st, axis=0, keepdims=True)
            p = jnp.exp2(st - m).astype(BF16)
            ot = jnp.dot(vt, p, preferred_element_type=F32)
            ot = ot[:hd] / ot[hd:hd + 1]
            outs += [ot[:, :rows].T, ot[:, rows:].T]
            st = st_next
        o_ref[...] = jnp.concatenate(outs, axis=1).astype(BF16)

    @pl.when(qt < CTX_LEN // ROW_TILE)
    def _():
        attend(CTX_LEN)

    @pl.when(qt >= CTX_LEN // ROW_TILE)
    def _():
        attend(k_ref.shape[1])


def _attention(aq, ak, avt):
    bsz, t_all, width = aq.shape
    qo = pl.BlockSpec((None, ROW_TILE, width), lambda b, t: (b, t, 0))
    k_spec = pl.BlockSpec((None, ATT_KV_HEADS, t_all, ATT_HEAD_DIM), lambda b, t: (b, 0, 0, 0))
    vt_spec = pl.BlockSpec((None, ATT_KV_HEADS, ATT_HEAD_DIM, t_all), lambda b, t: (b, 0, 0, 0))
    return pl.pallas_call(
        _attention_kernel,
        grid=(bsz, t_all // ROW_TILE),
        in_specs=[qo, k_spec, vt_spec],
        out_specs=qo,
        out_shape=jax.ShapeDtypeStruct((bsz, t_all, BRANCH_WIDTH), BF16),
        compiler_params=_cparams("parallel", "parallel"),
        name="prefix_gqa_attention",
    )(aq, ak, avt)


def _merge_kernel(x_ref, mod_ref, g_ref, yr_ref, yl_ref, ya_ref, wg_ref, wb_ref, wo_ref, o_ref):
    x = x_ref[...]
    hb = _modulated_norm(x, g_ref[...], mod_ref[3:4, :], mod_ref[4:5, :]).astype(BF16)
    d = x.shape[-1]
    mix = jnp.zeros(x.shape, F32)
    for n, y_ref in enumerate((yr_ref, yl_ref, ya_ref)):
        gate = jax.nn.sigmoid(jnp.dot(hb, wg_ref[:, n * d:(n + 1) * d], preferred_element_type=F32))
        mix = mix + gate * jnp.dot(y_ref[...], wb_ref[n], preferred_element_type=F32)
    y = jnp.dot(mix.astype(BF16), wo_ref[...], preferred_element_type=F32)
    o_ref[...] = x + mod_ref[5:6, :] * y


def _merge(xs, mods, norm_g, y_ret, y_lru, y_att, w_gate, w_branch, w_out, *, layer):
    bsz, t_all, d = xs.shape
    tile = lambda n: pl.BlockSpec((None, ROW_TILE, n), lambda b, t: (b, t, 0))
    return pl.pallas_call(
        _merge_kernel,
        grid=(bsz, t_all // ROW_TILE),
        in_specs=[tile(d),
                  _mod_spec(layer, d),
                  _pinned(norm_g.shape, layer, 1),
                  tile(BRANCH_WIDTH), tile(BRANCH_WIDTH), tile(BRANCH_WIDTH),
                  _pinned(w_gate.shape, layer), _pinned(w_branch.shape, layer), _pinned(w_out.shape, layer)],
        out_specs=tile(d),
        out_shape=jax.ShapeDtypeStruct(xs.shape, F32),
        compiler_params=_cparams("parallel", "parallel"),
        name="gated_branch_merge",
    )(xs, mods, norm_g, y_ret, y_lru, y_att, w_gate, w_branch, w_out)


def _rope_tables(seq, head_dim):
    half = head_dim // 2
    quarter = half // 2
    freqs = ROPE_THETA ** (-jnp.arange(0, half, 2, dtype=F32) / half)
    rows = jnp.repeat(jnp.arange(seq // GRID_W, dtype=F32), GRID_W)
    cols = jnp.tile(jnp.arange(GRID_W, dtype=F32), seq // GRID_W)
    ang_r = rows[:, None] * freqs[None]
    ang_c = cols[:, None] * freqs[None]
    ang = jnp.concatenate([ang_r, ang_r, ang_c, ang_c], axis=-1)
    sign = jnp.tile(jnp.concatenate([-jnp.ones((quarter,), F32), jnp.ones((quarter,), F32)]), 2)
    cos = jnp.tile(jnp.cos(ang), (1, LANES // head_dim))
    sin = jnp.tile(jnp.sin(ang) * sign[None], (1, LANES // head_dim))
    cos = jnp.concatenate([jnp.ones((CTX_LEN, LANES), F32), cos], axis=0)
    sin = jnp.concatenate([jnp.zeros((CTX_LEN, LANES), F32), sin], axis=0)
    return cos, sin


def _block_diag_lanes(w):
    depth = w.shape[0]
    per = LANES // LRU_BLOCK
    w = w.reshape(depth, 2, LRU_BLOCKS // per, per, LRU_BLOCK, LRU_BLOCK)
    eye = jnp.eye(per, dtype=w.dtype)
    dense = jnp.einsum('ldjpab,pq->ldjpaqb', w, eye).reshape(depth, 2, LRU_BLOCKS // per, LANES, LANES)
    return dense.transpose(0, 2, 1, 3, 4)


def kernel(x, c, ctx, c_ctx, w_mod, b_mod, norm_g, ffn_w_in, ffn_w_out, w_in,
           ret_decay_logit, ret_norm_g, lru_conv_w, lru_conv_b, lru_w_a, lru_b_a,
           lru_w_x, lru_b_x, lru_lambda, attn_q_norm_g, attn_k_norm_g, w_branch,
           w_out, final_norm_g):
    bsz, seq, d = x.shape
    depth = w_mod.shape[0]
    nb = LRU_WIDTH // LANES
    xs = jnp.concatenate([ctx, x], axis=1)

    c8 = jnp.concatenate([c, c_ctx[None], jnp.zeros((SUBLANES - bsz - 1, d), F32)], axis=0)
    m = _modulation(c8, w_mod, b_mod).reshape(depth, SUBLANES, 3 * N_SUB, d)
    m_ctx = jnp.broadcast_to(m[:, bsz:bsz + 1], (depth, bsz, 3 * N_SUB, d))
    mods = jnp.stack([m_ctx, m[:, :bsz]], axis=2).reshape(depth, 2 * bsz, 3 * N_SUB, d)

    g_all = norm_g.reshape(depth, N_SUB, 1, d)
    w1 = ffn_w_in.astype(BF16)
    w2 = ffn_w_out.astype(BF16)
    w_mix = w_in[:, :, :MIX_COLS].astype(BF16)
    w_gate = w_in[:, :, MIX_COLS:].astype(BF16)
    wb = w_branch.astype(BF16)
    wo = w_out.astype(BF16)
    rope = _rope_tables(seq, RET_HEAD_DIM) + _rope_tables(seq, ATT_HEAD_DIM)
    qg = jnp.tile(attn_q_norm_g, (1, 2))[:, None, :]
    kg = jnp.tile(attn_k_norm_g, (1, 2))[:, None, :]
    decay_lanes = jnp.broadcast_to(ret_decay_logit.reshape(depth, 2 * RET_HEADS, 1), (depth, 2 * RET_HEADS, LANES))
    ret_g = ret_norm_g[:, None, :]
    wa = _block_diag_lanes(lru_w_a)
    wx = _block_diag_lanes(lru_w_x)
    w_gates = jnp.concatenate([wa[:, :, 0], wx[:, :, 0], wa[:, :, 1], wx[:, :, 1]], axis=-1).astype(BF16)
    ba = lru_b_a.reshape(depth, 2, nb, LANES)
    bx = lru_b_x.reshape(depth, 2, nb, LANES)
    b_gates = jnp.concatenate([ba[:, 0], bx[:, 0], ba[:, 1], bx[:, 1]], axis=-1)[:, :, None, :]
    conv_b = lru_conv_b[:, None, :]
    final_g = final_norm_g[None]

    for l in range(depth):
        last = l == depth - 1
        xs = _ffn(xs, mods, g_all, w1, w2, layer=l, which=0)
        rq, rk, rv, rg, lx, lz, aq, ak, av = _proj(xs, mods, g_all, w_mix, rope, qg, kg, layer=l)
        y_ret = _retention(rq, rk, rv, rg, decay_lanes, ret_g, layer=l)
        y_lru = _lru(lx, lz, lru_conv_w, conv_b, w_gates, b_gates, lru_lambda, layer=l)
        y_att = _attention(aq, ak, av)
        xs = _merge(xs, mods, g_all, y_ret, y_lru, y_att, w_gate, wb, wo, layer=l)
        if last:
            xs = _ffn(xs, mods, g_all, w1, w2, layer=l, which=1, tile_off=CTX_LEN // ROW_TILE,
                      final_g=final_g)
        else:
            xs = _ffn(xs, mods, g_all, w1, w2, layer=l, which=1)
    return xs
```

```python
import functools

import jax
import jax.numpy as jnp
from jax import lax
from jax.experimental import pallas as pl
from jax.experimental.pallas import tpu as pltpu

D_MODEL = 1024
GRID_W = 64
CTX_LEN = 256
N_BRANCH = 3
BRANCH_WIDTH = 512
RET_HEADS = 4
RET_HEAD_DIM = 128
RET_CHUNK = 128
LRU_WIDTH = 512
LRU_BLOCKS = 8
LRU_BLOCK = 64
LRU_C = 8.0
ATT_HEAD_DIM = 64
ATT_Q_HEADS = 8
ATT_KV_HEADS = 2
ATT_GROUP = 4
ROPE_THETA = 10000.0
D_FF = 2816
MACARON_WEIGHT = 0.5
N_SUB = 3
NORM_EPS = 1e-6
LOG2_E = 1.4426950408889634
MIX_COLS = 4 * BRANCH_WIDTH + 2 * LRU_WIDTH + BRANCH_WIDTH + 2 * ATT_HEAD_DIM * ATT_KV_HEADS

LANES = 128
SUBLANES = 8
VMEM_LIMIT = 56 * 1024 * 1024

ROW_TILE = 256
FFN_CHUNK = 256
LRU_CHUNK = 256
RET_HEADS_PER_STEP = 2
RET_OUT_UNROLL = 8

BF16 = jnp.bfloat16
F32 = jnp.float32


def _cparams(*sem):
    return pltpu.CompilerParams(dimension_semantics=sem, vmem_limit_bytes=VMEM_LIMIT)


def _pinned(shape, *lead):
    tail = tuple(shape[len(lead):])
    index = tuple(lead) + (0,) * len(tail)
    return pl.BlockSpec((None,) * len(lead) + tail, lambda *_: index, pipeline_mode=pl.Buffered(1))


def _mod_spec(layer, d, tile_off=0):
    n_ctx_tiles = CTX_LEN // ROW_TILE
    return pl.BlockSpec(
        (None, None, 3 * N_SUB, d),
        lambda b, t: (layer, 2 * b + jnp.minimum((t + tile_off) // n_ctx_tiles, 1), 0, 0))


def _modulated_norm(x, g, shift, scale):
    ms = jnp.mean(x * x, axis=-1, keepdims=True)
    return x * lax.rsqrt(ms + NORM_EPS) * g * (1.0 + scale) + shift


def _mod_kernel(c_ref, w_ref, b_ref, o_ref):
    c = c_ref[...]
    s = (c * jax.nn.sigmoid(c)).astype(BF16)
    o_ref[...] = jnp.dot(s, w_ref[...].astype(BF16), preferred_element_type=F32) + b_ref[...]


def _modulation(c8, w_mod, b_mod):
    depth, d, n = w_mod.shape
    tn = 1024
    return pl.pallas_call(
        _mod_kernel,
        grid=(depth, n // tn),
        in_specs=[
            pl.BlockSpec((SUBLANES, d), lambda l, j: (0, 0)),
            pl.BlockSpec((None, d, tn), lambda l, j: (l, 0, j)),
            pl.BlockSpec((None, 1, tn), lambda l, j: (l, 0, j)),
        ],
        out_specs=pl.BlockSpec((None, SUBLANES, tn), lambda l, j: (l, 0, j)),
        out_shape=jax.ShapeDtypeStruct((depth, SUBLANES, n), F32),
        compiler_params=_cparams("parallel", "parallel"),
        name="adaln_modulation",
    )(c8, w_mod, b_mod.reshape(depth, 1, n))


def _ffn_kernel(x_ref, mod_ref, g_ref, w1_ref, w2_ref, *rest, sub, final_norm):
    if final_norm:
        fg_ref, o_ref = rest
    else:
        (o_ref,) = rest
    x = x_ref[...]
    shift = mod_ref[3 * sub + 0:3 * sub + 1, :]
    scale = mod_ref[3 * sub + 1:3 * sub + 2, :]
    gate = mod_ref[3 * sub + 2:3 * sub + 3, :]
    hb = _modulated_norm(x, g_ref[...], shift, scale).astype(BF16)
    acc = jnp.zeros(x.shape, F32)
    n_chunks = D_FF // FFN_CHUNK

    def up(c):
        lo = c * FFN_CHUNK
        return (jnp.dot(hb, w1_ref[:, lo:lo + FFN_CHUNK], preferred_element_type=F32),
                jnp.dot(hb, w1_ref[:, D_FF + lo:D_FF + lo + FFN_CHUNK], preferred_element_type=F32))

    ab = up(0)
    for c in range(n_chunks):
        ab_next = up(c + 1) if c + 1 < n_chunks else None
        a, b = ab
        act = (a * jax.nn.sigmoid(a) * b).astype(BF16)
        acc = acc + jnp.dot(act, w2_ref[c * FFN_CHUNK:(c + 1) * FFN_CHUNK, :], preferred_element_type=F32)
        ab = ab_next
    y = x + (MACARON_WEIGHT * gate) * acc
    if final_norm:
        ms = jnp.mean(y * y, axis=-1, keepdims=True)
        y = y * lax.rsqrt(ms + NORM_EPS) * fg_ref[...]
    o_ref[...] = y


def _ffn(xs, mods, norm_g, w1, w2, *, layer, which, tile_off=0, final_g=None):
    bsz, t_all, d = xs.shape
    sub = 2 * which
    nt = t_all // ROW_TILE - tile_off
    in_specs = [
        pl.BlockSpec((None, ROW_TILE, d), lambda b, t: (b, t + tile_off, 0)),
        _mod_spec(layer, d, tile_off),
        _pinned(norm_g.shape, layer, sub),
        _pinned(w1.shape, layer, which),
        _pinned(w2.shape, layer, which),
    ]
    args = [xs, mods, norm_g, w1, w2]
    if final_g is not None:
        in_specs.append(_pinned(final_g.shape))
        args.append(final_g)
    return pl.pallas_call(
        functools.partial(_ffn_kernel, sub=sub, final_norm=final_g is not None),
        grid=(bsz, nt),
        in_specs=in_specs,
        out_specs=pl.BlockSpec((None, ROW_TILE, d), lambda b, t: (b, t, 0)),
        out_shape=jax.ShapeDtypeStruct((bsz, nt * ROW_TILE, d), F32),
        compiler_params=_cparams("parallel", "parallel"),
        name="swiglu_half_step",
    )(*args)


def _rope_block(x, cos, sin_signed, half):
    lane = lax.broadcasted_iota(jnp.int32, x.shape, 1)
    up = pltpu.roll(x, LANES - half, axis=1)
    down = pltpu.roll(x, half, axis=1)
    partner = jnp.where((lane & half) == 0, up, down)
    return x * cos + partner * sin_signed


def _head_pair_rms(x, g):
    lane = lax.broadcasted_iota(jnp.int32, x.shape, 1)
    first = lane < ATT_HEAD_DIM
    sq = x * x
    s0 = jnp.sum(jnp.where(first, sq, 0.0), axis=-1, keepdims=True)
    s1 = jnp.sum(jnp.where(first, 0.0, sq), axis=-1, keepdims=True)
    ms = jnp.where(first, s0, s1) * (1.0 / ATT_HEAD_DIM)
    return x * lax.rsqrt(ms + NORM_EPS) * g


def _proj_kernel(x_ref, mod_ref, g_ref, w_ref, rcos_ref, rsin_ref, acos_ref, asin_ref,
                 qg_ref, kg_ref,
                 rq_ref, rk_ref, rv_ref, rg_ref, lx_ref, lz_ref, aq_ref, ak_ref, avt_ref):
    x = x_ref[...]
    hb = _modulated_norm(x, g_ref[...], mod_ref[3:4, :], mod_ref[4:5, :]).astype(BF16)
    w = BRANCH_WIDTH

    def cols(lo, n):
        return jnp.dot(hb, w_ref[:, lo:lo + n], preferred_element_type=F32)

    rcos, rsin = rcos_ref[...], rsin_ref[...]
    acos, asin = acos_ref[...], asin_ref[...]
    k_scale = RET_HEAD_DIM ** -0.5
    zq = cols(0, w)
    zk = cols(w, w)
    za = cols(6 * w, w)
    zkv = cols(7 * w, 2 * LANES)
    for h in range(RET_HEADS):
        sl = slice(h * LANES, (h + 1) * LANES)
        rq_ref[:, sl] = _rope_block(zq[:, sl], rcos, rsin, RET_HEAD_DIM // 4).astype(BF16)
        rk_ref[:, sl] = (_rope_block(zk[:, sl], rcos, rsin, RET_HEAD_DIM // 4) * k_scale).astype(BF16)
    rv_ref[...] = cols(2 * w, w).astype(BF16)
    rg_ref[...] = cols(3 * w, w)
    lx_ref[...] = cols(4 * w, w)
    lz_ref[...] = cols(5 * w, w)
    q_scale = ATT_HEAD_DIM ** -0.5 * LOG2_E
    for p in range(ATT_Q_HEADS // 2):
        sl = slice(p * LANES, (p + 1) * LANES)
        qn = _head_pair_rms(za[:, sl], qg_ref[...])
        aq_ref[:, sl] = (_rope_block(qn, acos, asin, ATT_HEAD_DIM // 4) * q_scale).astype(BF16)
    kn = _rope_block(_head_pair_rms(zkv[:, :LANES], kg_ref[...]), acos, asin, ATT_HEAD_DIM // 4)
    for j in range(ATT_KV_HEADS):
        ak_ref[j] = kn[:, j * ATT_HEAD_DIM:(j + 1) * ATT_HEAD_DIM].astype(BF16)
    vt = zkv[:, LANES:].T
    for j in range(ATT_KV_HEADS):
        avt_ref[j] = vt[j * ATT_HEAD_DIM:(j + 1) * ATT_HEAD_DIM, :].astype(BF16)


def _proj(xs, mods, norm_g, w_mix, rope, qg, kg, *, layer):
    bsz, t_all, d = xs.shape
    nt = t_all // ROW_TILE
    w = BRANCH_WIDTH
    tile = lambda n: pl.BlockSpec((None, ROW_TILE, n), lambda b, t: (b, t, 0))
    table = pl.BlockSpec((ROW_TILE, LANES), lambda b, t: (t, 0))
    kv_spec = pl.BlockSpec((None, ATT_KV_HEADS, ROW_TILE, ATT_HEAD_DIM), lambda b, t: (b, 0, t, 0))
    sds = lambda n, dt: jax.ShapeDtypeStruct((bsz, t_all, n), dt)
    kv_sds = jax.ShapeDtypeStruct((bsz, ATT_KV_HEADS, t_all, ATT_HEAD_DIM), BF16)
    vt_spec = pl.BlockSpec((None, ATT_KV_HEADS, ATT_HEAD_DIM, ROW_TILE), lambda b, t: (b, 0, 0, t))
    vt_sds = jax.ShapeDtypeStruct((bsz, ATT_KV_HEADS, ATT_HEAD_DIM, t_all), BF16)
    return pl.pallas_call(
        _proj_kernel,
        grid=(bsz, nt),
        in_specs=[
            tile(d),
            _mod_spec(layer, d),
            _pinned(norm_g.shape, layer, 1),
            _pinned(w_mix.shape, layer),
            table, table, table, table,
            _pinned(qg.shape, layer), _pinned(kg.shape, layer),
        ],
        out_specs=[tile(w), tile(w), tile(w), tile(w), tile(w), tile(w), tile(w), kv_spec, vt_spec],
        out_shape=[sds(w, BF16), sds(w, BF16), sds(w, BF16), sds(w, F32), sds(w, F32), sds(w, F32),
                   sds(w, BF16), kv_sds, vt_sds],
        compiler_params=_cparams("parallel", "parallel"),
        name="mixer_input_projection",
    )(xs, mods, norm_g, w_mix, *rope, qg, kg)


def _log_sigmoid(x):
    return jnp.minimum(x, 0.0) - jnp.log1p(jnp.exp(-jnp.abs(x)))


def _retention_kernel(q_ref, k_ref, v_ref, rg_ref, dl_ref, ng_ref, o_ref, s_ref):
    hb = pl.program_id(1)
    c = RET_CHUNK
    n_chunks = q_ref.shape[0] // c
    n_ctx = CTX_LEN // c
    row = lax.broadcasted_iota(jnp.int32, (c, c), 0).astype(F32)
    col = lax.broadcasted_iota(jnp.int32, (c, c), 1).astype(F32)
    diff = row - col

    consts = []
    for j in range(RET_HEADS_PER_STEP):
        h = hb * RET_HEADS_PER_STEP + j
        lg_f = _log_sigmoid(dl_ref[pl.ds(h, 1), :])
        lg_b = _log_sigmoid(dl_ref[pl.ds(RET_HEADS + h, 1), :])
        consts.append(dict(
            intra=(jnp.where(diff >= 0, jnp.exp(jnp.maximum(diff, 0.0) * lg_f), 0.0)
                   + jnp.where(diff <= 0, jnp.exp(jnp.maximum(-diff, 0.0) * lg_b), 0.0)),
            qd_f=jnp.exp((row + 1.0) * lg_f), kd_f=jnp.exp((c - 1.0 - row) * lg_f),
            qd_b=jnp.exp((c - row) * lg_b), kd_b=jnp.exp(row * lg_b),
            sd_f=jnp.exp(c * lg_f), sd_b=jnp.exp(c * lg_b)))

    def state_step(j, ci, s, kd, sd, lane0):
        r0 = pl.multiple_of(ci * c, c)
        sl = slice(j * LANES, (j + 1) * LANES)
        s_ref[j, ci, :, lane0:lane0 + LANES] = s.astype(BF16)
        kd_c = (k_ref[pl.ds(r0, c), sl].astype(F32) * kd).astype(BF16)
        kv = lax.dot_general(kd_c, v_ref[pl.ds(r0, c), sl], (((0,), (0,)), ((), ())),
                             preferred_element_type=F32)
        return s * sd + kv

    def pass1(i, carry):
        cb = jnp.where(i < n_ctx, n_ctx - 1 - i, n_chunks + n_ctx - 1 - i)
        out = []
        for j in range(RET_HEADS_PER_STEP):
            k = consts[j]
            s_f, s_b = carry[j]
            out.append((state_step(j, i, s_f, k["kd_f"], k["sd_f"], 0),
                        state_step(j, cb, s_b, k["kd_b"], k["sd_b"], LANES)))
        return tuple(out)

    zero = jnp.zeros((c, c), F32)
    lax.fori_loop(0, n_chunks, pass1, tuple((zero, zero) for _ in range(RET_HEADS_PER_STEP)), unroll=2)

    def pass2(ci, _):
        r0 = pl.multiple_of(ci * c, c)
        first = []
        for j in range(RET_HEADS_PER_STEP):
            sl = slice(j * LANES, (j + 1) * LANES)
            qc = q_ref[pl.ds(r0, c), sl]
            kc = k_ref[pl.ds(r0, c), sl]
            first.append((lax.dot_general(qc, kc, (((1,), (1,)), ((), ())), preferred_element_type=F32),
                          jnp.dot(qc, s_ref[j, ci], preferred_element_type=F32)))
        for j in range(RET_HEADS_PER_STEP):
            k = consts[j]
            sl = slice(j * LANES, (j + 1) * LANES)
            raw, cross = first[j]
            scores = raw * k["intra"]
            inner = jnp.dot(scores.astype(BF16), v_ref[pl.ds(r0, c), sl], preferred_element_type=F32)
            y = inner + cross[:, :LANES] * k["qd_f"] + cross[:, LANES:] * k["qd_b"]
            mu = jnp.mean(y, axis=-1, keepdims=True)
            yc = y - mu
            var = jnp.mean(yc * yc, axis=-1, keepdims=True)
            yn = yc * lax.rsqrt(var + NORM_EPS) * ng_ref[:, sl]
            gate = rg_ref[pl.ds(r0, c), sl]
            o_ref[pl.ds(r0, c), sl] = (yn * (gate * jax.nn.sigmoid(gate))).astype(BF16)
        return 0

    lax.fori_loop(0, n_chunks, pass2, 0, unroll=RET_OUT_UNROLL)


def _retention(rq, rk, rv, rg, decay_lanes, norm_g, *, layer):
    bsz, t_all, _ = rq.shape
    w = RET_HEADS_PER_STEP * LANES
    head = pl.BlockSpec((None, t_all, w), lambda b, h: (b, 0, h))
    return pl.pallas_call(
        _retention_kernel,
        grid=(bsz, RET_HEADS // RET_HEADS_PER_STEP),
        in_specs=[head, head, head, head,
                  pl.BlockSpec((None, 2 * RET_HEADS, LANES), lambda b, h: (layer, 0, 0)),
                  pl.BlockSpec((None, 1, w), lambda b, h: (layer, 0, h))],
        out_specs=head,
        out_shape=jax.ShapeDtypeStruct((bsz, t_all, BRANCH_WIDTH), BF16),
        scratch_shapes=[pltpu.VMEM((RET_HEADS_PER_STEP, t_all // RET_CHUNK, RET_HEAD_DIM, 2 * RET_HEAD_DIM), BF16)],
        compiler_params=_cparams("parallel", "parallel"),
        name="bidirectional_retention",
    )(rq, rk, rv, rg, decay_lanes, norm_g)


def _lru_kernel(lx_ref, lz_ref, cw_ref, cb_ref, w_ref, b_ref, lam_ref, o_ref,
                xp_ref, xs_ref, hf_ref, hb_ref):
    t_all = lx_ref.shape[0]
    ch = LRU_CHUNK
    n_chunks = t_all // ch
    pad = SUBLANES

    xp_ref[0:pad, :] = jnp.zeros((pad, LANES), F32)
    xp_ref[pad + t_all:pad + t_all + pad, :] = jnp.zeros((pad, LANES), F32)
    xp_ref[pad:pad + t_all, :] = lx_ref[...]
    cw = cw_ref[...]
    cb = cb_ref[...]
    for k in range(n_chunks):
        r0 = k * ch
        r = r0 + lax.broadcasted_iota(jnp.int32, (ch, LANES), 0)
        xm1 = xp_ref[pad + r0 - 1:pad + r0 - 1 + ch, :]
        x0 = xp_ref[pad + r0:pad + r0 + ch, :]
        xp1 = xp_ref[pad + r0 + 1:pad + r0 + 1 + ch, :]
        xp2 = xp_ref[pad + r0 + 2:pad + r0 + 2 + ch, :]
        if r0 <= CTX_LEN < r0 + ch:
            xm1 = jnp.where(r == CTX_LEN, 0.0, xm1)
        if r0 < CTX_LEN <= r0 + ch:
            xp1 = jnp.where(r == CTX_LEN - 1, 0.0, xp1)
            xp2 = jnp.where((r == CTX_LEN - 1) | (r == CTX_LEN - 2), 0.0, xp2)
        xs_ref[r0:r0 + ch, :] = cb + xm1 * cw[0:1] + x0 * cw[1:2] + xp1 * cw[2:3] + xp2 * cw[3:4]

    lam = lam_ref[...]
    sp = jnp.maximum(-lam, 0.0) + jnp.log1p(jnp.exp(-jnp.abs(lam)))
    n_groups = ch // SUBLANES
    sub = lax.broadcasted_iota(jnp.int32, (n_groups, SUBLANES, LANES), 1)

    def scan_chunk(ci, carry, d, out_ref):
        r0 = pl.multiple_of(ci * ch, ch)
        x = xs_ref[pl.ds(r0, ch), :]
        pre = jnp.dot(x.astype(BF16), w_ref[:, 2 * d * LANES:(2 * d + 2) * LANES],
                      preferred_element_type=F32) + b_ref[:, 2 * d * LANES:(2 * d + 2) * LANES]
        r_gate = jax.nn.sigmoid(pre[:, :LANES])
        i_gate = jax.nn.sigmoid(pre[:, LANES:])
        log_a = -LRU_C * r_gate * sp[d:d + 1]
        a = jnp.exp(log_a)
        th = jnp.tanh(log_a)
        u = jnp.sqrt(2.0 * th / (th - 1.0)) * (i_gate * x)
        a = a.reshape(n_groups, SUBLANES, LANES)
        u = u.reshape(n_groups, SUBLANES, LANES)
        for s in (1, 2, 4):
            if d == 0:
                a_sh = pltpu.roll(a, s, axis=1)
                u_sh = pltpu.roll(u, s, axis=1)
                ok = sub >= s
            else:
                a_sh = pltpu.roll(a, SUBLANES - s, axis=1)
                u_sh = pltpu.roll(u, SUBLANES - s, axis=1)
                ok = sub < SUBLANES - s
            u = jnp.where(ok, a * u_sh + u, u)
            a = jnp.where(ok, a * a_sh, a)
        groups = range(n_groups) if d == 0 else range(n_groups - 1, -1, -1)
        edge = SUBLANES - 1 if d == 0 else 0
        for gi in groups:
            hblk = u[gi] + a[gi] * carry
            out_ref[pl.ds(pl.multiple_of(r0 + gi * SUBLANES, SUBLANES), SUBLANES), :] = hblk
            carry = hblk[edge:edge + 1]
        return carry

    n_ctx = CTX_LEN // ch

    def body(i, carry):
        c_f, c_b = carry
        c_f = scan_chunk(i, c_f, 0, hf_ref)
        cb_idx = jnp.where(i < n_ctx, n_ctx - 1 - i, n_chunks + n_ctx - 1 - i)
        c_b = scan_chunk(cb_idx, c_b, 1, hb_ref)
        return c_f, c_b

    zero = jnp.zeros((1, LANES), F32)
    lax.fori_loop(0, n_chunks, body, (zero, zero))

    def finish(i, _):
        r0 = pl.multiple_of(i * ch, ch)
        z = lz_ref[pl.ds(r0, ch), :]
        hsum = hf_ref[pl.ds(r0, ch), :] + hb_ref[pl.ds(r0, ch), :]
        o_ref[pl.ds(r0, ch), :] = (jax.nn.gelu(z) * hsum).astype(BF16)
        return 0

    lax.fori_loop(0, n_chunks, finish, 0)


def _lru(lx, lz, conv_w, conv_b, w_gates, b_gates, lam, *, layer):
    bsz, t_all, width = lx.shape
    nb = width // LANES
    col = pl.BlockSpec((None, t_all, LANES), lambda b, j: (b, 0, j))
    return pl.pallas_call(
        _lru_kernel,
        grid=(bsz, nb),
        in_specs=[col, col,
                  pl.BlockSpec((None, 4, LANES), lambda b, j: (layer, 0, j)),
                  pl.BlockSpec((None, 1, LANES), lambda b, j: (layer, 0, j)),
                  pl.BlockSpec((None, None, LANES, 4 * LANES), lambda b, j: (layer, j, 0, 0)),
                  pl.BlockSpec((None, None, 1, 4 * LANES), lambda b, j: (layer, j, 0, 0)),
                  pl.BlockSpec((None, 2, LANES), lambda b, j: (layer, 0, j))],
        out_specs=col,
        out_shape=jax.ShapeDtypeStruct((bsz, t_all, width), BF16),
        scratch_shapes=[pltpu.VMEM((t_all + 2 * SUBLANES, LANES), F32)] + [pltpu.VMEM((t_all, LANES), F32)] * 3,
        compiler_params=_cparams("parallel", "parallel"),
        name="bidirectional_rglru",
    )(lx, lz, conv_w, conv_b, w_gates, b_gates, lam)


def _attention_kernel(q_ref, k_ref, vt_ref, o_ref):
    qt = pl.program_id(1)
    hd = ATT_HEAD_DIM
    rows = q_ref.shape[0]

    def attend(n_keys):
        n_pairs = ATT_Q_HEADS // 2

        def score(i):
            g = (2 * i) // ATT_GROUP
            q2 = jnp.concatenate([q_ref[:, (2 * i) * hd:(2 * i + 1) * hd],
                                  q_ref[:, (2 * i + 1) * hd:(2 * i + 2) * hd]], axis=0)
            return lax.dot_general(k_ref[g, 0:n_keys, :], q2, (((1,), (1,)), ((), ())),
                                   preferred_element_type=F32)

        outs = []
        st = score(0)
        for i in range(n_pairs):
            st_next = score(i + 1) if i + 1 < n_pairs else None
            g = (2 * i) // ATT_GROUP
            vt = jnp.concatenate([vt_ref[g, :, 0:n_keys], jnp.ones((2 * SUBLANES, n_keys), BF16)], axis=0)
            m = jnp.max(st, axis=0, keepdims=True)
            p = jnp.exp2(st - m).astype(BF16)
            ot = jnp.dot(vt, p, preferred_element_type=F32)
            ot = ot[:hd] / ot[hd:hd + 1]
            outs += [ot[:, :rows].T, ot[:, rows:].T]
            st = st_next
        o_ref[...] = jnp.concatenate(outs, axis=1).astype(BF16)

    @pl.when(qt < CTX_LEN // ROW_TILE)
    def _():
        attend(CTX_LEN)

    @pl.when(qt >= CTX_LEN // ROW_TILE)
    def _():
        attend(k_ref.shape[1])


def _attention(aq, ak, avt):
    bsz, t_all, width = aq.shape
    qo = pl.BlockSpec((None, ROW_TILE, width), lambda b, t: (b, t, 0))
    k_spec = pl.BlockSpec((None, ATT_KV_HEADS, t_all, ATT_HEAD_DIM), lambda b, t: (b, 0, 0, 0))
    vt_spec = pl.BlockSpec((None, ATT_KV_HEADS, ATT_HEAD_DIM, t_all), lambda b, t: (b, 0, 0, 0))
    return pl.pallas_call(
        _attention_kernel,
        grid=(bsz, t_all // ROW_TILE),
        in_specs=[qo, k_spec, vt_spec],
        out_specs=qo,
        out_shape=jax.ShapeDtypeStruct((bsz, t_all, BRANCH_WIDTH), BF16),
        compiler_params=_cparams("parallel", "parallel"),
        name="prefix_gqa_attention",
    )(aq, ak, avt)


def _merge_kernel(x_ref, mod_ref, g_ref, yr_ref, yl_ref, ya_ref, wg_ref, wb_ref, wo_ref, o_ref):
    x = x_ref[...]
    hb = _modulated_norm(x, g_ref[...], mod_ref[3:4, :], mod_ref[4:5, :]).astype(BF16)
    d = x.shape[-1]
    mix = jnp.zeros(x.shape, F32)
    for n, y_ref in enumerate((yr_ref, yl_ref, ya_ref)):
        gate = jax.nn.sigmoid(jnp.dot(hb, wg_ref[:, n * d:(n + 1) * d], preferred_element_type=F32))
        mix = mix + gate * jnp.dot(y_ref[...], wb_ref[n], preferred_element_type=F32)
    y = jnp.dot(mix.astype(BF16), wo_ref[...], preferred_element_type=F32)
    o_ref[...] = x + mod_ref[5:6, :] * y


def _merge(xs, mods, norm_g, y_ret, y_lru, y_att, w_gate, w_branch, w_out, *, layer):
    bsz, t_all, d = xs.shape
    tile = lambda n: pl.BlockSpec((None, ROW_TILE, n), lambda b, t: (b, t, 0))
    return pl.pallas_call(
        _merge_kernel,
        grid=(bsz, t_all // ROW_TILE),
        in_specs=[tile(d),
                  _mod_spec(layer, d),
                  _pinned(norm_g.shape, layer, 1),
                  tile(BRANCH_WIDTH), tile(BRANCH_WIDTH), tile(BRANCH_WIDTH),
                  _pinned(w_gate.shape, layer), _pinned(w_branch.shape, layer), _pinned(w_out.shape, layer)],
        out_specs=tile(d),
        out_shape=jax.ShapeDtypeStruct(xs.shape, F32),
        compiler_params=_cparams("parallel", "parallel"),
        name="gated_branch_merge",
    )(xs, mods, norm_g, y_ret, y_lru, y_att, w_gate, w_branch, w_out)


def _rope_tables(seq, head_dim):
    half = head_dim // 2
    quarter = half // 2
    freqs = ROPE_THETA ** (-jnp.arange(0, half, 2, dtype=F32) / half)
    rows = jnp.repeat(jnp.arange(seq // GRID_W, dtype=F32), GRID_W)
    cols = jnp.tile(jnp.arange(GRID_W, dtype=F32), seq // GRID_W)
    ang_r = rows[:, None] * freqs[None]
    ang_c = cols[:, None] * freqs[None]
    ang = jnp.concatenate([ang_r, ang_r, ang_c, ang_c], axis=-1)
    sign = jnp.tile(jnp.concatenate([-jnp.ones((quarter,), F32), jnp.ones((quarter,), F32)]), 2)
    cos = jnp.tile(jnp.cos(ang), (1, LANES // head_dim))
    sin = jnp.tile(jnp.sin(ang) * sign[None], (1, LANES // head_dim))
    cos = jnp.concatenate([jnp.ones((CTX_LEN, LANES), F32), cos], axis=0)
    sin = jnp.concatenate([jnp.zeros((CTX_LEN, LANES), F32), sin], axis=0)
    return cos, sin


def _block_diag_lanes(w):
    depth = w.shape[0]
    per = LANES // LRU_BLOCK
    w = w.reshape(depth, 2, LRU_BLOCKS // per, per, LRU_BLOCK, LRU_BLOCK)
    eye = jnp.eye(per, dtype=w.dtype)
    dense = jnp.einsum('ldjpab,pq->ldjpaqb', w, eye).reshape(depth, 2, LRU_BLOCKS // per, LANES, LANES)
    return dense.transpose(0, 2, 1, 3, 4)


def kernel(x, c, ctx, c_ctx, w_mod, b_mod, norm_g, ffn_w_in, ffn_w_out, w_in,
           ret_decay_logit, ret_norm_g, lru_conv_w, lru_conv_b, lru_w_a, lru_b_a,
           lru_w_x, lru_b_x, lru_lambda, attn_q_norm_g, attn_k_norm_g, w_branch,
           w_out, final_norm_g):
    bsz, seq, d = x.shape
    depth = w_mod.shape[0]
    nb = LRU_WIDTH // LANES
    xs = jnp.concatenate([ctx, x], axis=1)

    c8 = jnp.concatenate([c, c_ctx[None], jnp.zeros((SUBLANES - bsz - 1, d), F32)], axis=0)
    m = _modulation(c8, w_mod, b_mod).reshape(depth, SUBLANES, 3 * N_SUB, d)
    m_ctx = jnp.broadcast_to(m[:, bsz:bsz + 1], (depth, bsz, 3 * N_SUB, d))
    mods = jnp.stack([m_ctx, m[:, :bsz]], axis=2).reshape(depth, 2 * bsz, 3 * N_SUB, d)

    g_all = norm_g.reshape(depth, N_SUB, 1, d)
    w1 = ffn_w_in.astype(BF16)
    w2 = ffn_w_out.astype(BF16)
    w_mix = w_in[:, :, :MIX_COLS].astype(BF16)
    w_gate = w_in[:, :, MIX_COLS:].astype(BF16)
    wb = w_branch.astype(BF16)
    wo = w_out.astype(BF16)
    rope = _rope_tables(seq, RET_HEAD_DIM) + _rope_tables(seq, ATT_HEAD_DIM)
    qg = jnp.tile(attn_q_norm_g, (1, 2))[:, None, :]
    kg = jnp.tile(attn_k_norm_g, (1, 2))[:, None, :]
    decay_lanes = jnp.broadcast_to(ret_decay_logit.reshape(depth, 2 * RET_HEADS, 1), (depth, 2 * RET_HEADS, LANES))
    ret_g = ret_norm_g[:, None, :]
    wa = _block_diag_lanes(lru_w_a)
    wx = _block_diag_lanes(lru_w_x)
    w_gates = jnp.concatenate([wa[:, :, 0], wx[:, :, 0], wa[:, :, 1], wx[:, :, 1]], axis=-1).astype(BF16)
    ba = lru_b_a.reshape(depth, 2, nb, LANES)
    bx = lru_b_x.reshape(depth, 2, nb, LANES)
    b_gates = jnp.concatenate([ba[:, 0], bx[:, 0], ba[:, 1], bx[:, 1]], axis=-1)[:, :, None, :]
    conv_b = lru_conv_b[:, None, :]
    final_g = final_norm_g[None]

    for l in range(depth):
        last = l == depth - 1
        xs = _ffn(xs, mods, g_all, w1, w2, layer=l, which=0)
        rq, rk, rv, rg, lx, lz, aq, ak, av = _proj(xs, mods, g_all, w_mix, rope, qg, kg, layer=l)
        y_ret = _retention(rq, rk, rv, rg, decay_lanes, ret_g, layer=l)
        y_lru = _lru(lx, lz, lru_conv_w, conv_b, w_gates, b_gates, lru_lambda, layer=l)
        y_att = _attention(aq, ak, av)
        xs = _merge(xs, mods, g_all, y_ret, y_lru, y_att, w_gate, wb, wo, layer=l)
        if last:
            xs = _ffn(xs, mods, g_all, w1, w2, layer=l, which=1, tile_off=CTX_LEN // ROW_TILE,
                      final_g=final_g)
        else:
            xs = _ffn(xs, mods, g_all, w1, w2, layer=l, which=1)
    return xs
```

```python
import functools

import jax
import jax.numpy as jnp
from jax import lax
from jax.experimental import pallas as pl
from jax.experimental.pallas import tpu as pltpu

D_MODEL = 1024
GRID_W = 64
CTX_LEN = 256
N_BRANCH = 3
BRANCH_WIDTH = 512
RET_HEADS = 4
RET_HEAD_DIM = 128
RET_CHUNK = 128
LRU_WIDTH = 512
LRU_BLOCKS = 8
LRU_BLOCK = 64
LRU_C = 8.0
ATT_HEAD_DIM = 64
ATT_Q_HEADS = 8
ATT_KV_HEADS = 2
ATT_GROUP = 4
ROPE_THETA = 10000.0
D_FF = 2816
MACARON_WEIGHT = 0.5
N_SUB = 3
NORM_EPS = 1e-6
LOG2_E = 1.4426950408889634
MIX_COLS = 4 * BRANCH_WIDTH + 2 * LRU_WIDTH + BRANCH_WIDTH + 2 * ATT_HEAD_DIM * ATT_KV_HEADS

LANES = 128
SUBLANES = 8
VMEM_LIMIT = 56 * 1024 * 1024

ROW_TILE = 256
FFN_CHUNK = 256
LRU_CHUNK = 256
RET_HEADS_PER_STEP = 2
RET_OUT_UNROLL = 8

BF16 = jnp.bfloat16
F32 = jnp.float32


def _cparams(*sem):
    return pltpu.CompilerParams(dimension_semantics=sem, vmem_limit_bytes=VMEM_LIMIT)


def _pinned(shape, *lead):
    tail = tuple(shape[len(lead):])
    index = tuple(lead) + (0,) * len(tail)
    return pl.BlockSpec((None,) * len(lead) + tail, lambda *_: index, pipeline_mode=pl.Buffered(1))


def _mod_spec(layer, d, tile_off=0):
    n_ctx_tiles = CTX_LEN // ROW_TILE
    return pl.BlockSpec(
        (None, None, 3 * N_SUB, d),
        lambda b, t: (layer, 2 * b + jnp.minimum((t + tile_off) // n_ctx_tiles, 1), 0, 0))


def _modulated_norm(x, g, shift, scale):
    ms = jnp.mean(x * x, axis=-1, keepdims=True)
    return x * lax.rsqrt(ms + NORM_EPS) * g * (1.0 + scale) + shift


def _mod_kernel(c_ref, w_ref, b_ref, o_ref):
    c = c_ref[...]
    s = (c * jax.nn.sigmoid(c)).astype(BF16)
    o_ref[...] = jnp.dot(s, w_ref[...].astype(BF16), preferred_element_type=F32) + b_ref[...]


def _modulation(c8, w_mod, b_mod):
    depth, d, n = w_mod.shape
    tn = 1024
    return pl.pallas_call(
        _mod_kernel,
        grid=(depth, n // tn),
        in_specs=[
            pl.BlockSpec((SUBLANES, d), lambda l, j: (0, 0)),
            pl.BlockSpec((None, d, tn), lambda l, j: (l, 0, j)),
            pl.BlockSpec((None, 1, tn), lambda l, j: (l, 0, j)),
        ],
        out_specs=pl.BlockSpec((None, SUBLANES, tn), lambda l, j: (l, 0, j)),
        out_shape=jax.ShapeDtypeStruct((depth, SUBLANES, n), F32),
        compiler_params=_cparams("parallel", "parallel"),
        name="adaln_modulation",
    )(c8, w_mod, b_mod.reshape(depth, 1, n))


def _ffn_kernel(x_ref, mod_ref, g_ref, w1_ref, w2_ref, *rest, sub, final_norm):
    if final_norm:
        fg_ref, o_ref = rest
    else:
        (o_ref,) = rest
    x = x_ref[...]
    shift = mod_ref[3 * sub + 0:3 * sub + 1, :]
    scale = mod_ref[3 * sub + 1:3 * sub + 2, :]
    gate = mod_ref[3 * sub + 2:3 * sub + 3, :]
    hb = _modulated_norm(x, g_ref[...], shift, scale).astype(BF16)
    acc = jnp.zeros(x.shape, F32)
    n_chunks = D_FF // FFN_CHUNK

    def up(c):
        lo = c * FFN_CHUNK
        return (jnp.dot(hb, w1_ref[:, lo:lo + FFN_CHUNK], preferred_element_type=F32),
                jnp.dot(hb, w1_ref[:, D_FF + lo:D_FF + lo + FFN_CHUNK], preferred_element_type=F32))

    ab = up(0)
    for c in range(n_chunks):
        ab_next = up(c + 1) if c + 1 < n_chunks else None
        a, b = ab
        act = (a * jax.nn.sigmoid(a) * b).astype(BF16)
        acc = acc + jnp.dot(act, w2_ref[c * FFN_CHUNK:(c + 1) * FFN_CHUNK, :], preferred_element_type=F32)
        ab = ab_next
    y = x + (MACARON_WEIGHT * gate) * acc
    if final_norm:
        ms = jnp.mean(y * y, axis=-1, keepdims=True)
        y = y * lax.rsqrt(ms + NORM_EPS) * fg_ref[...]
    o_ref[...] = y


def _ffn(xs, mods, norm_g, w1, w2, *, layer, which, tile_off=0, final_g=None):
    bsz, t_all, d = xs.shape
    sub = 2 * which
    nt = t_all // ROW_TILE - tile_off
    in_specs = [
        pl.BlockSpec((None, ROW_TILE, d), lambda b, t: (b, t + tile_off, 0)),
        _mod_spec(layer, d, tile_off),
        _pinned(norm_g.shape, layer, sub),
        _pinned(w1.shape, layer, which),
        _pinned(w2.shape, layer, which),
    ]
    args = [xs, mods, norm_g, w1, w2]
    if final_g is not None:
        in_specs.append(_pinned(final_g.shape))
        args.append(final_g)
    return pl.pallas_call(
        functools.partial(_ffn_kernel, sub=sub, final_norm=final_g is not None),
        grid=(bsz, nt),
        in_specs=in_specs,
        out_specs=pl.BlockSpec((None, ROW_TILE, d), lambda b, t: (b, t, 0)),
        out_shape=jax.ShapeDtypeStruct((bsz, nt * ROW_TILE, d), F32),
        compiler_params=_cparams("parallel", "parallel"),
        name="swiglu_half_step",
    )(*args)


def _rope_block(x, cos, sin_signed, half):
    lane = lax.broadcasted_iota(jnp.int32, x.shape, 1)
    up = pltpu.roll(x, LANES - half, axis=1)
    down = pltpu.roll(x, half, axis=1)
    partner = jnp.where((lane & half) == 0, up, down)
    return x * cos + partner * sin_signed


def _head_pair_rms(x, g):
    lane = lax.broadcasted_iota(jnp.int32, x.shape, 1)
    first = lane < ATT_HEAD_DIM
    sq = x * x
    s0 = jnp.sum(jnp.where(first, sq, 0.0), axis=-1, keepdims=True)
    s1 = jnp.sum(jnp.where(first, 0.0, sq), axis=-1, keepdims=True)
    ms = jnp.where(first, s0, s1) * (1.0 / ATT_HEAD_DIM)
    return x * lax.rsqrt(ms + NORM_EPS) * g


def _proj_kernel(x_ref, mod_ref, g_ref, w_ref, rcos_ref, rsin_ref, acos_ref, asin_ref,
                 qg_ref, kg_ref,
                 rq_ref, rk_ref, rv_ref, rg_ref, lx_ref, lz_ref, aq_ref, ak_ref, avt_ref):
    x = x_ref[...]
    hb = _modulated_norm(x, g_ref[...], mod_ref[3:4, :], mod_ref[4:5, :]).astype(BF16)
    w = BRANCH_WIDTH

    def cols(lo, n):
        return jnp.dot(hb, w_ref[:, lo:lo + n], preferred_element_type=F32)

    rcos, rsin = rcos_ref[...], rsin_ref[...]
    acos, asin = acos_ref[...], asin_ref[...]
    k_scale = RET_HEAD_DIM ** -0.5
    zq = cols(0, w)
    zk = cols(w, w)
    za = cols(6 * w, w)
    zkv = cols(7 * w, 2 * LANES)
    for h in range(RET_HEADS):
        sl = slice(h * LANES, (h + 1) * LANES)
        rq_ref[:, sl] = _rope_block(zq[:, sl], rcos, rsin, RET_HEAD_DIM // 4).astype(BF16)
        rk_ref[:, sl] = (_rope_block(zk[:, sl], rcos, rsin, RET_HEAD_DIM // 4) * k_scale).astype(BF16)
    rv_ref[...] = cols(2 * w, w).astype(BF16)
    rg_ref[...] = cols(3 * w, w)
    lx_ref[...] = cols(4 * w, w)
    lz_ref[...] = cols(5 * w, w)
    q_scale = ATT_HEAD_DIM ** -0.5 * LOG2_E
    for p in range(ATT_Q_HEADS // 2):
        sl = slice(p * LANES, (p + 1) * LANES)
        qn = _head_pair_rms(za[:, sl], qg_ref[...])
        aq_ref[:, sl] = (_rope_block(qn, acos, asin, ATT_HEAD_DIM // 4) * q_scale).astype(BF16)
    kn = _rope_block(_head_pair_rms(zkv[:, :LANES], kg_ref[...]), acos, asin, ATT_HEAD_DIM // 4)
    for j in range(ATT_KV_HEADS):
        ak_ref[j] = kn[:, j * ATT_HEAD_DIM:(j + 1) * ATT_HEAD_DIM].astype(BF16)
    vt = zkv[:, LANES:].T
    for j in range(ATT_KV_HEADS):
        avt_ref[j] = vt[j * ATT_HEAD_DIM:(j + 1) * ATT_HEAD_DIM, :].astype(BF16)


def _proj(xs, mods, norm_g, w_mix, rope, qg, kg, *, layer):
    bsz, t_all, d = xs.shape
    nt = t_all // ROW_TILE
    w = BRANCH_WIDTH
    tile = lambda n: pl.BlockSpec((None, ROW_TILE, n), lambda b, t: (b, t, 0))
    table = pl.BlockSpec((ROW_TILE, LANES), lambda b, t: (t, 0))
    kv_spec = pl.BlockSpec((None, ATT_KV_HEADS, ROW_TILE, ATT_HEAD_DIM), lambda b, t: (b, 0, t, 0))
    sds = lambda n, dt: jax.ShapeDtypeStruct((bsz, t_all, n), dt)
    kv_sds = jax.ShapeDtypeStruct((bsz, ATT_KV_HEADS, t_all, ATT_HEAD_DIM), BF16)
    vt_spec = pl.BlockSpec((None, ATT_KV_HEADS, ATT_HEAD_DIM, ROW_TILE), lambda b, t: (b, 0, 0, t))
    vt_sds = jax.ShapeDtypeStruct((bsz, ATT_KV_HEADS, ATT_HEAD_DIM, t_all), BF16)
    return pl.pallas_call(
        _proj_kernel,
        grid=(bsz, nt),
        in_specs=[
            tile(d),
            _mod_spec(layer, d),
            _pinned(norm_g.shape, layer, 1),
            _pinned(w_mix.shape, layer),
            table, table, table, table,
            _pinned(qg.shape, layer), _pinned(kg.shape, layer),
        ],
        out_specs=[tile(w), tile(w), tile(w), tile(w), tile(w), tile(w), tile(w), kv_spec, vt_spec],
        out_shape=[sds(w, BF16), sds(w, BF16), sds(w, BF16), sds(w, F32), sds(w, F32), sds(w, F32),
                   sds(w, BF16), kv_sds, vt_sds],
        compiler_params=_cparams("parallel", "parallel"),
        name="mixer_input_projection",
    )(xs, mods, norm_g, w_mix, *rope, qg, kg)


def _log_sigmoid(x):
    return jnp.minimum(x, 0.0) - jnp.log1p(jnp.exp(-jnp.abs(x)))


def _retention_kernel(q_ref, k_ref, v_ref, rg_ref, dl_ref, ng_ref, o_ref, s_ref):
    hb = pl.program_id(1)
    c = RET_CHUNK
    n_chunks = q_ref.shape[0] // c
    n_ctx = CTX_LEN // c
    row = lax.broadcasted_iota(jnp.int32, (c, c), 0).astype(F32)
    col = lax.broadcasted_iota(jnp.int32, (c, c), 1).astype(F32)
    diff = row - col

    consts = []
    for j in range(RET_HEADS_PER_STEP):
        h = hb * RET_HEADS_PER_STEP + j
        lg_f = _log_sigmoid(dl_ref[pl.ds(h, 1), :])
        lg_b = _log_sigmoid(dl_ref[pl.ds(RET_HEADS + h, 1), :])
        consts.append(dict(
            intra=(jnp.where(diff >= 0, jnp.exp(jnp.maximum(diff, 0.0) * lg_f), 0.0)
                   + jnp.where(diff <= 0, jnp.exp(jnp.maximum(-diff, 0.0) * lg_b), 0.0)),
            qd_f=jnp.exp((row + 1.0) * lg_f), kd_f=jnp.exp((c - 1.0 - row) * lg_f),
            qd_b=jnp.exp((c - row) * lg_b), kd_b=jnp.exp(row * lg_b),
            sd_f=jnp.exp(c * lg_f), sd_b=jnp.exp(c * lg_b)))

    def state_step(j, ci, s, kd, sd, lane0):
        r0 = pl.multiple_of(ci * c, c)
        sl = slice(j * LANES, (j + 1) * LANES)
        s_ref[j, ci, :, lane0:lane0 + LANES] = s.astype(BF16)
        kd_c = (k_ref[pl.ds(r0, c), sl].astype(F32) * kd).astype(BF16)
        kv = lax.dot_general(kd_c, v_ref[pl.ds(r0, c), sl], (((0,), (0,)), ((), ())),
                             preferred_element_type=F32)
        return s * sd + kv

    def pass1(i, carry):
        cb = jnp.where(i < n_ctx, n_ctx - 1 - i, n_chunks + n_ctx - 1 - i)
        out = []
        for j in range(RET_HEADS_PER_STEP):
            k = consts[j]
            s_f, s_b = carry[j]
            out.append((state_step(j, i, s_f, k["kd_f"], k["sd_f"], 0),
                        state_step(j, cb, s_b, k["kd_b"], k["sd_b"], LANES)))
        return tuple(out)

    zero = jnp.zeros((c, c), F32)
    lax.fori_loop(0, n_chunks, pass1, tuple((zero, zero) for _ in range(RET_HEADS_PER_STEP)), unroll=8)

    def pass2(ci, _):
        r0 = pl.multiple_of(ci * c, c)
        first = []
        for j in range(RET_HEADS_PER_STEP):
            sl = slice(j * LANES, (j + 1) * LANES)
            qc = q_ref[pl.ds(r0, c), sl]
            kc = k_ref[pl.ds(r0, c), sl]
            first.append((lax.dot_general(qc, kc, (((1,), (1,)), ((), ())), preferred_element_type=F32),
                          jnp.dot(qc, s_ref[j, ci], preferred_element_type=F32)))
        for j in range(RET_HEADS_PER_STEP):
            k = consts[j]
            sl = slice(j * LANES, (j + 1) * LANES)
            raw, cross = first[j]
            scores = raw * k["intra"]
            inner = jnp.dot(scores.astype(BF16), v_ref[pl.ds(r0, c), sl], preferred_element_type=F32)
            y = inner + cross[:, :LANES] * k["qd_f"] + cross[:, LANES:] * k["qd_b"]
            mu = jnp.mean(y, axis=-1, keepdims=True)
            yc = y - mu
            var = jnp.mean(yc * yc, axis=-1, keepdims=True)
            yn = yc * lax.rsqrt(var + NORM_EPS) * ng_ref[:, sl]
            gate = rg_ref[pl.ds(r0, c), sl]
            o_ref[pl.ds(r0, c), sl] = (yn * (gate * jax.nn.sigmoid(gate))).astype(BF16)
        return 0

    lax.fori_loop(0, n_chunks, pass2, 0, unroll=RET_OUT_UNROLL)


def _retention(rq, rk, rv, rg, decay_lanes, norm_g, *, layer):
    bsz, t_all, _ = rq.shape
    w = RET_HEADS_PER_STEP * LANES
    head = pl.BlockSpec((None, t_all, w), lambda b, h: (b, 0, h))
    return pl.pallas_call(
        _retention_kernel,
        grid=(bsz, RET_HEADS // RET_HEADS_PER_STEP),
        in_specs=[head, head, head, head,
                  pl.BlockSpec((None, 2 * RET_HEADS, LANES), lambda b, h: (layer, 0, 0)),
                  pl.BlockSpec((None, 1, w), lambda b, h: (layer, 0, h))],
        out_specs=head,
        out_shape=jax.ShapeDtypeStruct((bsz, t_all, BRANCH_WIDTH), BF16),
        scratch_shapes=[pltpu.VMEM((RET_HEADS_PER_STEP, t_all // RET_CHUNK, RET_HEAD_DIM, 2 * RET_HEAD_DIM), BF16)],
        compiler_params=_cparams("parallel", "parallel"),
        name="bidirectional_retention",
    )(rq, rk, rv, rg, decay_lanes, norm_g)


def _lru_kernel(lx_ref, lz_ref, cw_ref, cb_ref, w_ref, b_ref, lam_ref, o_ref,
                xp_ref, xs_ref, hf_ref, hb_ref):
    t_all = lx_ref.shape[0]
    ch = LRU_CHUNK
    n_chunks = t_all // ch
    pad = SUBLANES

    xp_ref[0:pad, :] = jnp.zeros((pad, LANES), F32)
    xp_ref[pad + t_all:pad + t_all + pad, :] = jnp.zeros((pad, LANES), F32)
    xp_ref[pad:pad + t_all, :] = lx_ref[...]
    cw = cw_ref[...]
    cb = cb_ref[...]
    for k in range(n_chunks):
        r0 = k * ch
        r = r0 + lax.broadcasted_iota(jnp.int32, (ch, LANES), 0)
        xm1 = xp_ref[pad + r0 - 1:pad + r0 - 1 + ch, :]
        x0 = xp_ref[pad + r0:pad + r0 + ch, :]
        xp1 = xp_ref[pad + r0 + 1:pad + r0 + 1 + ch, :]
        xp2 = xp_ref[pad + r0 + 2:pad + r0 + 2 + ch, :]
        if r0 <= CTX_LEN < r0 + ch:
            xm1 = jnp.where(r == CTX_LEN, 0.0, xm1)
        if r0 < CTX_LEN <= r0 + ch:
            xp1 = jnp.where(r == CTX_LEN - 1, 0.0, xp1)
            xp2 = jnp.where((r == CTX_LEN - 1) | (r == CTX_LEN - 2), 0.0, xp2)
        xs_ref[r0:r0 + ch, :] = cb + xm1 * cw[0:1] + x0 * cw[1:2] + xp1 * cw[2:3] + xp2 * cw[3:4]

    lam = lam_ref[...]
    sp = jnp.maximum(-lam, 0.0) + jnp.log1p(jnp.exp(-jnp.abs(lam)))
    n_groups = ch // SUBLANES
    sub = lax.broadcasted_iota(jnp.int32, (n_groups, SUBLANES, LANES), 1)

    def scan_chunk(ci, carry, d, out_ref):
        r0 = pl.multiple_of(ci * ch, ch)
        x = xs_ref[pl.ds(r0, ch), :]
        pre = jnp.dot(x.astype(BF16), w_ref[:, 2 * d * LANES:(2 * d + 2) * LANES],
                      preferred_element_type=F32) + b_ref[:, 2 * d * LANES:(2 * d + 2) * LANES]
        r_gate = jax.nn.sigmoid(pre[:, :LANES])
        i_gate = jax.nn.sigmoid(pre[:, LANES:])
        log_a = -LRU_C * r_gate * sp[d:d + 1]
        a = jnp.exp(log_a)
        th = jnp.tanh(log_a)
        u = jnp.sqrt(2.0 * th / (th - 1.0)) * (i_gate * x)
        a = a.reshape(n_groups, SUBLANES, LANES)
        u = u.reshape(n_groups, SUBLANES, LANES)
        for s in (1, 2, 4):
            if d == 0:
                a_sh = pltpu.roll(a, s, axis=1)
                u_sh = pltpu.roll(u, s, axis=1)
                ok = sub >= s
            else:
                a_sh = pltpu.roll(a, SUBLANES - s, axis=1)
                u_sh = pltpu.roll(u, SUBLANES - s, axis=1)
                ok = sub < SUBLANES - s
            u = jnp.where(ok, a * u_sh + u, u)
            a = jnp.where(ok, a * a_sh, a)
        groups = range(n_groups) if d == 0 else range(n_groups - 1, -1, -1)
        edge = SUBLANES - 1 if d == 0 else 0
        for gi in groups:
            hblk = u[gi] + a[gi] * carry
            out_ref[pl.ds(pl.multiple_of(r0 + gi * SUBLANES, SUBLANES), SUBLANES), :] = hblk
            carry = hblk[edge:edge + 1]
        return carry

    n_ctx = CTX_LEN // ch

    def body(i, carry):
        c_f, c_b = carry
        c_f = scan_chunk(i, c_f, 0, hf_ref)
        cb_idx = jnp.where(i < n_ctx, n_ctx - 1 - i, n_chunks + n_ctx - 1 - i)
        c_b = scan_chunk(cb_idx, c_b, 1, hb_ref)
        return c_f, c_b

    zero = jnp.zeros((1, LANES), F32)
    lax.fori_loop(0, n_chunks, body, (zero, zero), unroll=2)

    def finish(i, _):
        r0 = pl.multiple_of(i * ch, ch)
        z = lz_ref[pl.ds(r0, ch), :]
        hsum = hf_ref[pl.ds(r0, ch), :] + hb_ref[pl.ds(r0, ch), :]
        o_ref[pl.ds(r0, ch), :] = (jax.nn.gelu(z) * hsum).astype(BF16)
        return 0

    lax.fori_loop(0, n_chunks, finish, 0)


def _lru(lx, lz, conv_w, conv_b, w_gates, b_gates, lam, *, layer):
    bsz, t_all, width = lx.shape
    nb = width // LANES
    col = pl.BlockSpec((None, t_all, LANES), lambda b, j: (b, 0, j))
    return pl.pallas_call(
        _lru_kernel,
        grid=(bsz, nb),
        in_specs=[col, col,
                  pl.BlockSpec((None, 4, LANES), lambda b, j: (layer, 0, j)),
                  pl.BlockSpec((None, 1, LANES), lambda b, j: (layer, 0, j)),
                  pl.BlockSpec((None, None, LANES, 4 * LANES), lambda b, j: (layer, j, 0, 0)),
                  pl.BlockSpec((None, None, 1, 4 * LANES), lambda b, j: (layer, j, 0, 0)),
                  pl.BlockSpec((None, 2, LANES), lambda b, j: (layer, 0, j))],
        out_specs=col,
        out_shape=jax.ShapeDtypeStruct((bsz, t_all, width), BF16),
        scratch_shapes=[pltpu.VMEM((t_all + 2 * SUBLANES, LANES), F32)] + [pltpu.VMEM((t_all, LANES), F32)] * 3,
        compiler_params=_cparams("parallel", "parallel"),
        name="bidirectional_rglru",
    )(lx, lz, conv_w, conv_b, w_gates, b_gates, lam)


def _attention_kernel(q_ref, k_ref, vt_ref, o_ref):
    qt = pl.program_id(1)
    hd = ATT_HEAD_DIM
    rows = q_ref.shape[0]

    def attend(n_keys):
        n_pairs = ATT_Q_HEADS // 2

        def score(i):
            g = (2 * i) // ATT_GROUP
            q2 = jnp.concatenate([q_ref[:, (2 * i) * hd:(2 * i + 1) * hd],
                                  q_ref[:, (2 * i + 1) * hd:(2 * i + 2) * hd]], axis=0)
            return lax.dot_general(k_ref[g, 0:n_keys, :], q2, (((1,), (1,)), ((), ())),
                                   preferred_element_type=F32)

        outs = []
        st = score(0)
        for i in range(n_pairs):
            st_next = score(i + 1) if i + 1 < n_pairs else None
            g = (2 * i) // ATT_GROUP
            vt = jnp.concatenate([vt_ref[g, :, 0:n_keys], jnp.ones((2 * SUBLANES, n_keys), BF16)], axis=0)
            m = jnp.max(st, axis=0, keepdims=True)
            p = jnp.exp2(st - m).astype(BF16)
            ot = jnp.dot(vt, p, preferred_element_type=F32)
            ot = ot[:hd] / ot[hd:hd + 1]
            outs += [ot[:, :rows].T, ot[:, rows:].T]
            st = st_next
        o_ref[...] = jnp.concatenate(outs, axis=1).astype(BF16)

    @pl.when(qt < CTX_LEN // ROW_TILE)
    def _():
        attend(CTX_LEN)

    @pl.when(qt >= CTX_LEN // ROW_TILE)
    def _():
        attend(k_ref.shape[1])


def _attention(aq, ak, avt):
    bsz, t_all, width = aq.shape
    qo = pl.BlockSpec((None, ROW_TILE, width), lambda b, t: (b, t, 0))
    k_spec = pl.BlockSpec((None, ATT_KV_HEADS, t_all, ATT_HEAD_DIM), lambda b, t: (b, 0, 0, 0))
    vt_spec = pl.BlockSpec((None, ATT_KV_HEADS, ATT_HEAD_DIM, t_all), lambda b, t: (b, 0, 0, 0))
    return pl.pallas_call(
        _attention_kernel,
        grid=(bsz, t_all // ROW_TILE),
        in_specs=[qo, k_spec, vt_spec],
        out_specs=qo,
        out_shape=jax.ShapeDtypeStruct((bsz, t_all, BRANCH_WIDTH), BF16),
        compiler_params=_cparams("parallel", "parallel"),
        name="prefix_gqa_attention",
    )(aq, ak, avt)


def _merge_kernel(x_ref, mod_ref, g_ref, yr_ref, yl_ref, ya_ref, wg_ref, wb_ref, wo_ref, o_ref):
    x = x_ref[...]
    hb = _modulated_norm(x, g_ref[...], mod_ref[3:4, :], mod_ref[4:5, :]).astype(BF16)
    d = x.shape[-1]
    mix = jnp.zeros(x.shape, F32)
    for n, y_ref in enumerate((yr_ref, yl_ref, ya_ref)):
        gate = jax.nn.sigmoid(jnp.dot(hb, wg_ref[:, n * d:(n + 1) * d], preferred_element_type=F32))
        mix = mix + gate * jnp.dot(y_ref[...], wb_ref[n], preferred_element_type=F32)
    y = jnp.dot(mix.astype(BF16), wo_ref[...], preferred_element_type=F32)
    o_ref[...] = x + mod_ref[5:6, :] * y


def _merge(xs, mods, norm_g, y_ret, y_lru, y_att, w_gate, w_branch, w_out, *, layer):
    bsz, t_all, d = xs.shape
    tile = lambda n: pl.BlockSpec((None, ROW_TILE, n), lambda b, t: (b, t, 0))
    return pl.pallas_call(
        _merge_kernel,
        grid=(bsz, t_all // ROW_TILE),
        in_specs=[tile(d),
                  _mod_spec(layer, d),
                  _pinned(norm_g.shape, layer, 1),
                  tile(BRANCH_WIDTH), tile(BRANCH_WIDTH), tile(BRANCH_WIDTH),
                  _pinned(w_gate.shape, layer), _pinned(w_branch.shape, layer), _pinned(w_out.shape, layer)],
        out_specs=tile(d),
        out_shape=jax.ShapeDtypeStruct(xs.shape, F32),
        compiler_params=_cparams("parallel", "parallel"),
        name="gated_branch_merge",
    )(xs, mods, norm_g, y_ret, y_lru, y_att, w_gate, w_branch, w_out)


def _rope_tables(seq, head_dim):
    half = head_dim // 2
    quarter = half // 2
    freqs = ROPE_THETA ** (-jnp.arange(0, half, 2, dtype=F32) / half)
    rows = jnp.repeat(jnp.arange(seq // GRID_W, dtype=F32), GRID_W)
    cols = jnp.tile(jnp.arange(GRID_W, dtype=F32), seq // GRID_W)
    ang_r = rows[:, None] * freqs[None]
    ang_c = cols[:, None] * freqs[None]
    ang = jnp.concatenate([ang_r, ang_r, ang_c, ang_c], axis=-1)
    sign = jnp.tile(jnp.concatenate([-jnp.ones((quarter,), F32), jnp.ones((quarter,), F32)]), 2)
    cos = jnp.tile(jnp.cos(ang), (1, LANES // head_dim))
    sin = jnp.tile(jnp.sin(ang) * sign[None], (1, LANES // head_dim))
    cos = jnp.concatenate([jnp.ones((CTX_LEN, LANES), F32), cos], axis=0)
    sin = jnp.concatenate([jnp.zeros((CTX_LEN, LANES), F32), sin], axis=0)
    return cos, sin


def _block_diag_lanes(w):
    depth = w.shape[0]
    per = LANES // LRU_BLOCK
    w = w.reshape(depth, 2, LRU_BLOCKS // per, per, LRU_BLOCK, LRU_BLOCK)
    eye = jnp.eye(per, dtype=w.dtype)
    dense = jnp.einsum('ldjpab,pq->ldjpaqb', w, eye).reshape(depth, 2, LRU_BLOCKS // per, LANES, LANES)
    return dense.transpose(0, 2, 1, 3, 4)


def kernel(x, c, ctx, c_ctx, w_mod, b_mod, norm_g, ffn_w_in, ffn_w_out, w_in,
           ret_decay_logit, ret_norm_g, lru_conv_w, lru_conv_b, lru_w_a, lru_b_a,
           lru_w_x, lru_b_x, lru_lambda, attn_q_norm_g, attn_k_norm_g, w_branch,
           w_out, final_norm_g):
    bsz, seq, d = x.shape
    depth = w_mod.shape[0]
    nb = LRU_WIDTH // LANES
    xs = jnp.concatenate([ctx, x], axis=1)

    c8 = jnp.concatenate([c, c_ctx[None], jnp.zeros((SUBLANES - bsz - 1, d), F32)], axis=0)
    m = _modulation(c8, w_mod, b_mod).reshape(depth, SUBLANES, 3 * N_SUB, d)
    m_ctx = jnp.broadcast_to(m[:, bsz:bsz + 1], (depth, bsz, 3 * N_SUB, d))
    mods = jnp.stack([m_ctx, m[:, :bsz]], axis=2).reshape(depth, 2 * bsz, 3 * N_SUB, d)

    g_all = norm_g.reshape(depth, N_SUB, 1, d)
    w1 = ffn_w_in.astype(BF16)
    w2 = ffn_w_out.astype(BF16)
    w_mix = w_in[:, :, :MIX_COLS].astype(BF16)
    w_gate = w_in[:, :, MIX_COLS:].astype(BF16)
    wb = w_branch.astype(BF16)
    wo = w_out.astype(BF16)
    rope = _rope_tables(seq, RET_HEAD_DIM) + _rope_tables(seq, ATT_HEAD_DIM)
    qg = jnp.tile(attn_q_norm_g, (1, 2))[:, None, :]
    kg = jnp.tile(attn_k_norm_g, (1, 2))[:, None, :]
    decay_lanes = jnp.broadcast_to(ret_decay_logit.reshape(depth, 2 * RET_HEADS, 1), (depth, 2 * RET_HEADS, LANES))
    ret_g = ret_norm_g[:, None, :]
    wa = _block_diag_lanes(lru_w_a)
    wx = _block_diag_lanes(lru_w_x)
    w_gates = jnp.concatenate([wa[:, :, 0], wx[:, :, 0], wa[:, :, 1], wx[:, :, 1]], axis=-1).astype(BF16)
    ba = lru_b_a.reshape(depth, 2, nb, LANES)
    bx = lru_b_x.reshape(depth, 2, nb, LANES)
    b_gates = jnp.concatenate([ba[:, 0], bx[:, 0], ba[:, 1], bx[:, 1]], axis=-1)[:, :, None, :]
    conv_b = lru_conv_b[:, None, :]
    final_g = final_norm_g[None]

    for l in range(depth):
        last = l == depth - 1
        xs = _ffn(xs, mods, g_all, w1, w2, layer=l, which=0)
        rq, rk, rv, rg, lx, lz, aq, ak, av = _proj(xs, mods, g_all, w_mix, rope, qg, kg, layer=l)
        y_ret = _retention(rq, rk, rv, rg, decay_lanes, ret_g, layer=l)
        y_lru = _lru(lx, lz, lru_conv_w, conv_b, w_gates, b_gates, lru_lambda, layer=l)
        y_att = _attention(aq, ak, av)
        xs = _merge(xs, mods, g_all, y_ret, y_lru, y_att, w_gate, wb, wo, layer=l)
        if last:
            xs = _ffn(xs, mods, g_all, w1, w2, layer=l, which=1, tile_off=CTX_LEN // ROW_TILE,
                      final_g=final_g)
        else:
            xs = _ffn(xs, mods, g_all, w1, w2, layer=l, which=1)
    return xs
```
